```python
import math
import jax
import jax.numpy as jnp
from jax import lax
import numpy as np

D_MODEL = 1024
BATCH = 16
SEQ = 2048
DEPTH = 2

N_MIXERS = 2
DN_HEADS = 8
DN_HEAD_DIM = 128
DN_WIDTH = DN_HEADS * DN_HEAD_DIM
DN_CONV = 5
DN_CHUNK = 64
DN_IN = 4 * DN_WIDTH + 4 * DN_HEADS
CF_CONV = 31
N_EXPERTS = 16
EXPERT_FF = 1024
EC_CAPACITY_FACTOR = 2
EPS = 1e-6
N_A = (DEPTH + 1) // 2
N_B = DEPTH // 2

kernel_name = 'hybrid_gdn_conformer_ec_moe'


def rms_norm(x, g):
    x32 = x.astype(jnp.float32)
    y = x32 * lax.rsqrt(jnp.mean(x32 * x32, axis=-1, keepdims=True) + EPS)
    return (y * g.astype(jnp.float32)).astype(x.dtype)


def layer_norm(x, g, b):
    x32 = x.astype(jnp.float32)
    xc = x32 - jnp.mean(x32, axis=-1, keepdims=True)
    y = xc * lax.rsqrt(jnp.mean(xc * xc, axis=-1, keepdims=True) + EPS)
    return (y * g.astype(jnp.float32) + b.astype(jnp.float32)).astype(x.dtype)


def l2_normalize(a):
    return a * lax.rsqrt(jnp.sum(a * a, axis=-1, keepdims=True) + EPS)


def centred_depthwise_conv(x, w):
    width, ch = w.shape
    pad = width // 2
    return lax.conv_general_dilated(
        x, w[:, None, :].astype(x.dtype), window_strides=(1,), padding=[(pad, pad)],
        dimension_numbers=('NWC', 'WIO', 'NWC'), feature_group_count=ch)


def chunk_gated_delta_rule(q, k, v, log_g, beta):
    b, h, t, dk = q.shape
    dv = v.shape[-1]
    c = DN_CHUNK
    n = t // c
    q, k, v = [a.reshape(b, h, n, c, a.shape[-1]) for a in (q, k, v)]
    beta = beta.reshape(b, h, n, c)
    gc = jnp.cumsum(log_g.reshape(b, h, n, c), axis=-1)
    idx = jnp.arange(c)
    incl = idx[:, None] >= idx[None, :]
    strict = idx[:, None] > idx[None, :]
    decay = jnp.exp(jnp.where(incl, gc[..., :, None] - gc[..., None, :], -jnp.inf))
    kb = k * beta[..., None]
    lower = jnp.where(strict, jnp.einsum('bhncd,bhnsd->bhncs', kb, k) * decay, 0.0)
    eye = jnp.eye(c, dtype=q.dtype)
    rhs = jnp.concatenate([v * beta[..., None], kb * jnp.exp(gc)[..., None]], axis=-1)
    sol = lax.linalg.triangular_solve(eye + lower, rhs, left_side=True, lower=True, unit_diagonal=True)
    u, w = sol[..., :dv], sol[..., dv:]
    intra = jnp.einsum('bhncd,bhnsd->bhncs', q, k) * decay

    def step(state, inp):
        q_i, k_i, u_i, w_i, intra_i, g_i = inp
        v_new = u_i - jnp.einsum('bhcd,bhde->bhce', w_i, state)
        o = (jnp.einsum('bhcd,bhde->bhce', q_i * jnp.exp(g_i)[..., None], state)
             + jnp.einsum('bhcs,bhse->bhce', intra_i, v_new))
        g_last = g_i[..., -1]
        state = (state * jnp.exp(g_last)[..., None, None]
                 + jnp.einsum('bhcd,bhce->bhde', k_i * jnp.exp(g_last[..., None] - g_i)[..., None], v_new))
        return state, o

    xs = tuple(jnp.moveaxis(a, 2, 0) for a in (q, k, u, w, intra, gc))
    state0 = jnp.zeros((b, h, dk, dv), q.dtype)
    _, o = lax.scan(step, state0, xs)
    return jnp.moveaxis(o, 0, 2).reshape(b, h, t, dv)


def gated_deltanet_mixer(hn, w_in, conv_w, a_log, dt_bias, norm_w, w_out):
    b, t, _ = hn.shape
    f32 = jnp.float32
    proj = hn @ w_in
    qkv, z, gates = jnp.split(proj, [3 * DN_WIDTH, 4 * DN_WIDTH], axis=-1)
    qkv = jax.nn.silu(centred_depthwise_conv(qkv, conv_w))
    q, k, v = [a.reshape(b, t, DN_HEADS, DN_HEAD_DIM).transpose(0, 2, 1, 3).astype(f32)
               for a in jnp.split(qkv, 3, axis=-1)]
    q = l2_normalize(q) * (DN_HEAD_DIM ** -0.5)
    k = l2_normalize(k)
    gates = gates.astype(f32).reshape(b, t, 2, 2, DN_HEADS)
    beta = jax.nn.sigmoid(gates[:, :, :, 0, :]).transpose(2, 0, 3, 1)
    log_g = (-jnp.exp(a_log.astype(f32)) * jax.nn.softplus(gates[:, :, :, 1, :] + dt_bias.astype(f32))
             ).transpose(2, 0, 3, 1)
    flip = lambda a: jnp.flip(a, axis=2)
    o_fwd = chunk_gated_delta_rule(q, k, v, log_g[0], beta[0])
    o_bwd = flip(chunk_gated_delta_rule(flip(q), flip(k), flip(v), flip(log_g[1]), flip(beta[1])))
    o = (o_fwd + o_bwd).transpose(0, 2, 1, 3)
    zz = z.reshape(b, t, DN_HEADS, DN_HEAD_DIM).astype(f32)
    o = o * lax.rsqrt(jnp.mean(o * o, axis=-1, keepdims=True) + EPS) * norm_w.astype(f32) * jax.nn.silu(zz)
    return o.reshape(b, t, DN_WIDTH).astype(hn.dtype) @ w_out


def conformer_conv_mixer(hn, w_pw1, b_pw1, dw_w, dw_b, ln_g, ln_b, w_pw2, b_pw2):
    a, gate = jnp.split(hn @ w_pw1 + b_pw1, 2, axis=-1)
    u = a * jax.nn.sigmoid(gate)
    u = centred_depthwise_conv(u, dw_w) + dw_b
    u = jax.nn.silu(layer_norm(u, ln_g, ln_b))
    return u @ w_pw2 + b_pw2


def expert_choice_moe(hn, w_router, w_gate, w_up, w_down):
    b, t, _ = hn.shape
    cap = EC_CAPACITY_FACTOR * t // N_EXPERTS
    probs = jax.nn.softmax((hn @ w_router).astype(jnp.float32), axis=-1)
    aff, tok = lax.top_k(probs.transpose(0, 2, 1), cap)
    bidx = jnp.arange(b)[:, None, None]
    xin = hn[bidx, tok]
    hid = (jax.nn.silu(jnp.einsum('becd,edf->becf', xin, w_gate))
           * jnp.einsum('becd,edf->becf', xin, w_up))
    y = jnp.einsum('becf,efd->becd', hid, w_down) * aff[..., None].astype(hn.dtype)
    return jnp.zeros_like(hn).at[bidx, tok].add(y)


def setup_inputs(seed: int = 0) -> dict:
    key = jax.random.key(seed)
    ks = jax.random.split(key, 24)
    d = D_MODEL

    def nrm(k, shape, scale):
        return jax.random.normal(k, shape, jnp.float32) * scale

    def gain(k, shape):
        return 1.0 + 0.02 * jax.random.normal(k, shape, jnp.float32)

    dt = jnp.exp(jax.random.uniform(ks[6], (N_A, 2, DN_HEADS), jnp.float32,
                                    math.log(1e-3), math.log(1e-1)))
    return {
        'x': nrm(ks[0], (BATCH, SEQ, d), 1.0),
        'norm_mix': gain(ks[1], (DEPTH, d)),
        'norm_ffn': gain(ks[2], (DEPTH, d)),
        'dn_w_in': nrm(ks[3], (N_A, d, DN_IN), d ** -0.5),
        'dn_conv_w': nrm(ks[4], (N_A, DN_CONV, 3 * DN_WIDTH), DN_CONV ** -0.5),
        'dn_a_log': jnp.log(jax.random.uniform(ks[5], (N_A, 2, DN_HEADS), jnp.float32, 1.0, 16.0)),
        'dn_dt_bias': dt + jnp.log(-jnp.expm1(-dt)),
        'dn_norm_w': gain(ks[7], (N_A, DN_HEAD_DIM)),
        'dn_w_out': nrm(ks[8], (N_A, DN_WIDTH, d), DN_WIDTH ** -0.5),
        'cf_w_pw1': nrm(ks[9], (N_B, d, 2 * d), d ** -0.5),
        'cf_b_pw1': nrm(ks[10], (N_B, 2 * d), 0.01),
        'cf_dw_w': nrm(ks[11], (N_B, CF_CONV, d), CF_CONV ** -0.5),
        'cf_dw_b': nrm(ks[12], (N_B, d), 0.01),
        'cf_ln_g': gain(ks[13], (N_B, d)),
        'cf_ln_b': nrm(ks[14], (N_B, d), 0.01),
        'cf_w_pw2': nrm(ks[15], (N_B, d, d), d ** -0.5),
        'cf_b_pw2': nrm(ks[16], (N_B, d), 0.01),
        'moe_w_router': nrm(ks[17], (DEPTH, d, N_EXPERTS), d ** -0.5),
        'moe_w_gate': nrm(ks[18], (DEPTH, N_EXPERTS, d, EXPERT_FF), d ** -0.5),
        'moe_w_up': nrm(ks[19], (DEPTH, N_EXPERTS, d, EXPERT_FF), d ** -0.5),
        'moe_w_down': nrm(ks[20], (DEPTH, N_EXPERTS, EXPERT_FF, d), EXPERT_FF ** -0.5),
        'final_norm': gain(ks[21], (d,)),
    }


def reference(x, norm_mix, norm_ffn, dn_w_in, dn_conv_w, dn_a_log, dn_dt_bias, dn_norm_w, dn_w_out,
              cf_w_pw1, cf_b_pw1, cf_dw_w, cf_dw_b, cf_ln_g, cf_ln_b, cf_w_pw2, cf_b_pw2,
              moe_w_router, moe_w_gate, moe_w_up, moe_w_down, final_norm):
    for i in range(DEPTH):
        j = i // N_MIXERS
        hn = rms_norm(x, norm_mix[i])
        if i % N_MIXERS == 0:
            mix = gated_deltanet_mixer(hn, dn_w_in[j], dn_conv_w[j], dn_a_log[j], dn_dt_bias[j],
                                       dn_norm_w[j], dn_w_out[j])
        else:
            mix = conformer_conv_mixer(hn, cf_w_pw1[j], cf_b_pw1[j], cf_dw_w[j], cf_dw_b[j],
                                       cf_ln_g[j], cf_ln_b[j], cf_w_pw2[j], cf_b_pw2[j])
        x = x + mix
        x = x + expert_choice_moe(rms_norm(x, norm_ffn[i]), moe_w_router[i], moe_w_gate[i],
                                  moe_w_up[i], moe_w_down[i])
    return rms_norm(x, final_norm)
```

```python
import functools

import jax
import jax.numpy as jnp
from jax import lax
from jax.experimental import pallas as pl
from jax.experimental.pallas import tpu as pltpu

F32 = jnp.float32
BF16 = jnp.bfloat16
EPS = 1e-6

LANES = 128
HEADS = 8
HEAD_DIM = 128
DN_CONV = 5
CF_CONV = 31
N_EXPERTS = 16
CAPACITY_FACTOR = 2
GDN_CHUNK = 128
GDN_GROUP = 4
VMEM_LIMIT = 56 * 1024 * 1024


def _params(*sem):
    return pltpu.CompilerParams(dimension_semantics=sem, vmem_limit_bytes=VMEM_LIMIT)


def _sigmoid(x):
    return 1.0 / (1.0 + jnp.exp(-x))


def _silu(x):
    return x * _sigmoid(x)


def _softplus(x):
    return jnp.maximum(x, 0.0) + jnp.log1p(jnp.exp(-jnp.abs(x)))


def _rms(x, w):
    return x * lax.rsqrt(jnp.mean(x * x, axis=-1, keepdims=True) + EPS) * w


def _dot(a, b):
    return jnp.dot(a, b, preferred_element_type=F32)


def _bdot(spec, a, b):
    return jnp.einsum(spec, a, b, preferred_element_type=F32)


def _seg_cumsum(x, seg, reverse):
    n = x.shape[0]
    r = lax.broadcasted_iota(jnp.int32, (n, 1), 0) % seg
    s = 1
    while s < seg:
        if reverse:
            x = x + jnp.where(r < seg - s, pltpu.roll(x, n - s, axis=0), 0.0)
        else:
            x = x + jnp.where(r >= s, pltpu.roll(x, s, axis=0), 0.0)
        s *= 2
    return x


def _halo_specs(tm, halo, nt_total, d):
    per = tm // halo
    prev = pl.BlockSpec((1, halo, d), lambda b, t: (b, jnp.maximum(t * per - 1, 0), 0))
    cur = pl.BlockSpec((1, tm, d), lambda b, t: (b, t, 0))
    nxt = pl.BlockSpec((1, halo, d), lambda b, t: (b, jnp.minimum((t + 1) * per, nt_total - 1), 0))
    return prev, cur, nxt


def _ext_valid(tm, halo, nt):
    t = pl.program_id(1)
    rows = lax.broadcasted_iota(jnp.int32, (tm + 2 * halo, 1), 0)
    lo = jnp.where(t == 0, halo, 0)
    hi = jnp.where(t == nt - 1, tm + halo, tm + 2 * halo)
    return (rows >= lo) & (rows < hi)


def _gdn_pre_kernel(xp_ref, xc_ref, xn_ref, nw_ref, wqkvz_ref, wg_ref, cw_ref, alog_ref, dtb_ref,
                    q_ref, k_ref, v_ref, z_ref, g_ref, ext_ref, *, tm, nt, halo, chunk):
    width = HEADS * HEAD_DIM
    x_ext = jnp.concatenate([xp_ref[0], xc_ref[0], xn_ref[0]], axis=0)
    hn = _rms(x_ext, nw_ref[...]).astype(BF16)
    proj = _dot(hn, wqkvz_ref[...])
    valid = _ext_valid(tm, halo, nt)
    ext_ref[...] = jnp.where(valid, proj[:, :3 * width], 0.0)
    z_ref[0] = proj[halo:halo + tm, 3 * width:]

    conv = jnp.zeros((tm, 3 * width), F32)
    for j in range(DN_CONV):
        conv = conv + cw_ref[j:j + 1, :] * ext_ref[pl.ds(halo - DN_CONV // 2 + j, tm), :]
    act = _silu(conv)
    for h in range(HEADS):
        sl = slice(h * HEAD_DIM, (h + 1) * HEAD_DIM)
        qh = act[:, sl]
        kh = act[:, width + h * HEAD_DIM: width + (h + 1) * HEAD_DIM]
        q_ref[0, :, sl] = (qh * lax.rsqrt(jnp.sum(qh * qh, axis=-1, keepdims=True) + EPS)
                           * (HEAD_DIM ** -0.5)).astype(BF16)
        k_ref[0, :, sl] = (kh * lax.rsqrt(jnp.sum(kh * kh, axis=-1, keepdims=True) + EPS)).astype(BF16)
    v_ref[0] = act[:, 2 * width:].astype(BF16)

    graw = _dot(hn[halo:halo + tm], wg_ref[...])
    lane = lax.broadcasted_iota(jnp.int32, (1, LANES), 1)
    beta = _sigmoid(graw)
    logg = -jnp.exp(alog_ref[...]) * _softplus(graw + dtb_ref[...])
    cum_f = _seg_cumsum(logg, chunk, reverse=False)
    cum_b = _seg_cumsum(logg, chunk, reverse=True)
    is_beta = ((lane % (2 * HEADS)) < HEADS) & (lane < 4 * HEADS)
    is_f = (lane >= HEADS) & (lane < 2 * HEADS)
    is_b = (lane >= 3 * HEADS) & (lane < 4 * HEADS)
    g_ref[0] = jnp.where(is_beta, beta, jnp.where(is_f, cum_f, jnp.where(is_b, cum_b, 0.0)))


def _gdn_pre(x, norm_w, w_in, conv_w, a_log, dt_bias, *, tm=256, halo=8):
    bsz, t, d = x.shape
    width = HEADS * HEAD_DIM
    nt = t // tm
    wqkvz = w_in[:, :4 * width].astype(BF16)
    wg = jnp.pad(w_in[:, 4 * width:], ((0, 0), (0, LANES - 4 * HEADS))).astype(BF16)
    cw = jnp.pad(conv_w, ((0, 8 - DN_CONV), (0, 0)))
    zeros = jnp.zeros((HEADS,), F32)
    alog = jnp.pad(jnp.concatenate([zeros, a_log[0], zeros, a_log[1]]), (0, LANES - 4 * HEADS))[None]
    dtb = jnp.pad(jnp.concatenate([zeros, dt_bias[0], zeros, dt_bias[1]]), (0, LANES - 4 * HEADS))[None]
    prev, cur, nxt = _halo_specs(tm, halo, t // halo, d)
    full = lambda shape: pl.BlockSpec(shape, lambda b, tt: (0,) * len(shape))
    tile = lambda w: pl.BlockSpec((1, tm, w), lambda b, tt: (b, tt, 0))
    return pl.pallas_call(
        functools.partial(_gdn_pre_kernel, tm=tm, nt=nt, halo=halo, chunk=GDN_CHUNK),
        grid=(bsz, nt),
        in_specs=[prev, cur, nxt, full((1, d)), full((d, 4 * width)), full((d, LANES)),
                  full((8, 3 * width)), full((1, LANES)), full((1, LANES))],
        out_specs=[tile(width), tile(width), tile(width), tile(width), tile(LANES)],
        out_shape=[jax.ShapeDtypeStruct((bsz, t, width), BF16)] * 3
        + [jax.ShapeDtypeStruct((bsz, t, width), F32), jax.ShapeDtypeStruct((bsz, t, LANES), F32)],
        scratch_shapes=[pltpu.VMEM((tm + 2 * halo, 3 * width), F32)],
        compiler_params=_params("parallel", "parallel"),
        name="gdn_pre",
    )(x, x, x, norm_w[None], wqkvz, wg, cw, alog, dtb)


def _unit_tri_inverse(a, c):
    ii = lax.broadcasted_iota(jnp.int32, (c, c), 0)
    jj = lax.broadcasted_iota(jnp.int32, (c, c), 1)
    eye = (ii == jj).astype(F32)
    x = eye - jnp.where((ii // 2) == (jj // 2), a, 0.0)
    s = 2
    while s < c:
        off = ((ii // (2 * s)) == (jj // (2 * s))) & ((ii // s) != (jj // s))
        a_l = jnp.where(off, a, 0.0).astype(BF16)
        y = _bdot('nij,njk->nik', x.astype(BF16), a_l)
        x = x - _bdot('nij,njk->nik', y.astype(BF16), x.astype(BF16))
        s *= 2
    return x


def _gdn_core_kernel(q_ref, k_ref, v_ref, z_ref, g_ref, grow_ref, nw_ref, o_ref,
                     m_scr, n_scr, qp_scr, s_scr, oacc_scr, *, t, c, ng):
    h = pl.program_id(1)
    n = t // c
    rg = ng * c
    d = HEAD_DIM
    lane = lax.broadcasted_iota(jnp.int32, (1, LANES), 1)
    ii = lax.broadcasted_iota(jnp.int32, (c, c), 0)
    jj = lax.broadcasted_iota(jnp.int32, (c, c), 1)
    eye_d = (lax.broadcasted_iota(jnp.int32, (d, d), 0) == lax.broadcasted_iota(jnp.int32, (d, d), 1)).astype(F32)

    def precompute(g, carry):
        r0 = pl.multiple_of(g * rg, rg)
        c0 = pl.multiple_of(g * ng, ng)
        gt = g_ref[0, pl.ds(r0, rg), :]
        kb = k_ref[0, pl.ds(r0, rg), :].reshape(ng, c, d)
        qb = q_ref[0, pl.ds(r0, rg), :].reshape(ng, c, d)
        kf = kb.astype(F32)
        qf = qb.astype(F32)
        vf = v_ref[0, pl.ds(r0, rg), :].astype(F32).reshape(ng, c, d)
        kk = _bdot('ncd,nsd->ncs', kb, kb)
        qk = _bdot('ncd,nsd->ncs', qb, kb)
        for di in range(2):
            col = lambda idx: jnp.sum(jnp.where(lane == idx * HEADS + h, gt, 0.0), axis=-1,
                                      keepdims=True).reshape(ng, c, 1)
            beta = col(2 * di)
            gcol = col(2 * di + 1)
            grow = grow_ref[0, 2 * di + 1, 0, pl.ds(c0, ng)]
            incl = (ii >= jj) if di == 0 else (ii <= jj)
            strict = (ii > jj) if di == 0 else (ii < jj)
            decay = jnp.where(incl, jnp.exp(jnp.where(incl, gcol - grow, 0.0)), 0.0)
            a = jnp.where(strict, beta * kk * decay, 0.0)
            tinv = _unit_tri_inverse(a, c)
            egc = jnp.exp(gcol)
            rhs = jnp.concatenate([vf * beta, kf * (beta * egc)], axis=-1).astype(BF16)
            uw = _bdot('ncs,nsd->ncd', tinv.astype(BF16), rhs).astype(BF16)
            iw = _bdot('ncs,nsd->ncd', (qk * decay).astype(BF16), uw)
            qp = qf * egc - iw[..., d:]
            glast = gcol[:, c - 1:c, :] if di == 0 else gcol[:, 0:1, :]
            kt = (kf * jnp.exp(glast - gcol)).astype(BF16)
            ktuw = _bdot('ncd,nce->nde', kt, uw)
            m_scr[di, pl.ds(c0, ng)] = (jnp.exp(glast) * eye_d - ktuw[..., d:]).astype(BF16)
            n_scr[di, pl.ds(c0, ng)] = ktuw[..., :d]
            qp_scr[di, pl.ds(r0, rg), :] = qp.reshape(rg, d).astype(BF16)
            if di == 0:
                oacc_scr[pl.ds(r0, rg), :] = iw[..., :d].reshape(rg, d)
            else:
                oacc_scr[pl.ds(r0, rg), :] += iw[..., :d].reshape(rg, d)
        return carry

    lax.fori_loop(0, n // ng, precompute, 0)

    s_scr[0, 0] = jnp.zeros((d, d), BF16)
    s_scr[1, n - 1] = jnp.zeros((d, d), BF16)

    def scan(i, carry):
        nb = n - 1 - i
        s_scr[0, i + 1] = (_dot(m_scr[0, i], s_scr[0, i]) + n_scr[0, i]).astype(BF16)
        s_scr[1, nb - 1] = (_dot(m_scr[1, nb], s_scr[1, nb]) + n_scr[1, nb]).astype(BF16)
        return carry

    lax.fori_loop(0, n - 1, scan, 0)

    def finish(g, carry):
        r0 = pl.multiple_of(g * rg, rg)
        c0 = pl.multiple_of(g * ng, ng)
        o = oacc_scr[pl.ds(r0, rg), :]
        for di in range(2):
            qp = qp_scr[di, pl.ds(r0, rg), :].reshape(ng, c, d)
            o = o + _bdot('ncd,nde->nce', qp, s_scr[di, pl.ds(c0, ng)]).reshape(rg, d)
        zf = z_ref[0, pl.ds(r0, rg), :]
        o_ref[0, pl.ds(r0, rg), :] = (_rms(o, nw_ref[...]) * _silu(zf)).astype(BF16)
        return carry

    lax.fori_loop(0, n // ng, finish, 0)


def _gdn_core(q, k, v, z, g, norm_w):
    bsz, t, width = q.shape
    c, ng, d = GDN_CHUNK, GDN_GROUP, HEAD_DIM
    n = t // c
    grow = g[:, :, :4 * HEADS].reshape(bsz, n, c, 4, HEADS).transpose(0, 3, 4, 1, 2)[:, :, :, :, None, :]
    head = pl.BlockSpec((1, t, d), lambda b, h: (b, 0, h))
    return pl.pallas_call(
        functools.partial(_gdn_core_kernel, t=t, c=c, ng=ng),
        grid=(bsz, HEADS),
        in_specs=[head, head, head, head,
                  pl.BlockSpec((1, t, LANES), lambda b, h: (b, 0, 0)),
                  pl.BlockSpec((1, 4, 1, n, 1, c), lambda b, h: (b, 0, h, 0, 0, 0)),
                  pl.BlockSpec((1, d), lambda b, h: (0, 0))],
        out_specs=head,
        out_shape=jax.ShapeDtypeStruct((bsz, t, width), BF16),
        scratch_shapes=[pltpu.VMEM((2, n, d, d), BF16), pltpu.VMEM((2, n, d, d), F32),
                        pltpu.VMEM((2, t, d), BF16), pltpu.VMEM((2, n, d, d), BF16),
                        pltpu.VMEM((t, d), F32)],
        compiler_params=_params("parallel", "parallel"),
        name="gdn_core",
    )(q, k, v, z, g, grow, norm_w[None])


def _router_epilogue(x1, nffn_ref, wr_ref, hn_ref, lg_ref):
    hn = _rms(x1, nffn_ref[...])
    hi = hn.astype(BF16)
    lo = (hn - hi.astype(F32)).astype(BF16)
    hn_ref[0] = hi
    lg_ref[0] = _dot(hi, wr_ref[0]) + _dot(hi, wr_ref[1]) + _dot(lo, wr_ref[0])


def _router_weights(w_router):
    wp = jnp.pad(w_router, ((0, 0), (0, LANES - N_EXPERTS)))
    hi = wp.astype(BF16)
    lo = (wp - hi.astype(F32)).astype(BF16)
    return jnp.stack([hi, lo])


def _outproj_kernel(og_ref, x_ref, wout_ref, nffn_ref, wr_ref, x1_ref, hn_ref, lg_ref):
    x1 = x_ref[0] + _dot(og_ref[0], wout_ref[...])
    x1_ref[0] = x1
    _router_epilogue(x1, nffn_ref, wr_ref, hn_ref, lg_ref)


def _outproj_router(og, x, w_out, norm_ffn, w_router, *, tm=512):
    bsz, t, d = x.shape
    tile = lambda w: pl.BlockSpec((1, tm, w), lambda b, tt: (b, tt, 0))
    full = lambda shape: pl.BlockSpec(shape, lambda b, tt: (0,) * len(shape))
    return pl.pallas_call(
        _outproj_kernel,
        grid=(bsz, t // tm),
        in_specs=[tile(og.shape[-1]), tile(d), full(w_out.shape), full((1, d)), full((2, d, LANES))],
        out_specs=[tile(d), tile(d), tile(LANES)],
        out_shape=[jax.ShapeDtypeStruct((bsz, t, d), F32), jax.ShapeDtypeStruct((bsz, t, d), BF16),
                   jax.ShapeDtypeStruct((bsz, t, LANES), F32)],
        compiler_params=_params("parallel", "parallel"),
        name="outproj_router",
    )(og, x, w_out.astype(BF16), norm_ffn[None], _router_weights(w_router))


def _conformer_kernel(xp_ref, xc_ref, xn_ref, nmix_ref, w1_ref, b1_ref, dww_ref, dwb_ref, lng_ref, lnb_ref,
                      w2_ref, b2_ref, nffn_ref, wr_ref, x1_ref, hn_ref, lg_ref, u_scr, *, tm, nt, halo):
    d = xc_ref.shape[-1]
    x_ext = jnp.concatenate([xp_ref[0], xc_ref[0], xn_ref[0]], axis=0)
    hn = _rms(x_ext, nmix_ref[...]).astype(BF16)
    pg = _dot(hn, w1_ref[...]) + b1_ref[...]
    u = pg[:, :d] * _sigmoid(pg[:, d:])
    u_scr[...] = jnp.where(_ext_valid(tm, halo, nt), u, 0.0)
    conv = jnp.zeros((tm, d), F32)
    for j in range(CF_CONV):
        conv = conv + dww_ref[j:j + 1, :] * u_scr[pl.ds(halo - CF_CONV // 2 + j, tm), :]
    conv = conv + dwb_ref[...]
    xc = conv - jnp.mean(conv, axis=-1, keepdims=True)
    y = xc * lax.rsqrt(jnp.mean(xc * xc, axis=-1, keepdims=True) + EPS) * lng_ref[...] + lnb_ref[...]
    mix = _dot(_silu(y).astype(BF16), w2_ref[...]) + b2_ref[...]
    x1 = xc_ref[0] + mix
    x1_ref[0] = x1
    _router_epilogue(x1, nffn_ref, wr_ref, hn_ref, lg_ref)


def _conformer_router(x, norm_mix, w_pw1, b_pw1, dw_w, dw_b, ln_g, ln_b, w_pw2, b_pw2, norm_ffn, w_router,
                      *, tm=256, halo=16):
    bsz, t, d = x.shape
    nt = t // tm
    prev, cur, nxt = _halo_specs(tm, halo, t // halo, d)
    full = lambda shape: pl.BlockSpec(shape, lambda b, tt: (0,) * len(shape))
    tile = lambda w: pl.BlockSpec((1, tm, w), lambda b, tt: (b, tt, 0))
    dww = jnp.pad(dw_w, ((0, 32 - CF_CONV), (0, 0)))
    return pl.pallas_call(
        functools.partial(_conformer_kernel, tm=tm, nt=nt, halo=halo),
        grid=(bsz, nt),
        in_specs=[prev, cur, nxt, full((1, d)), full((d, 2 * d)), full((1, 2 * d)), full((32, d)),
                  full((1, d)), full((1, d)), full((1, d)), full((d, d)), full((1, d)), full((1, d)),
                  full((2, d, LANES))],
        out_specs=[tile(d), tile(d), tile(LANES)],
        out_shape=[jax.ShapeDtypeStruct((bsz, t, d), F32), jax.ShapeDtypeStruct((bsz, t, d), BF16),
                   jax.ShapeDtypeStruct((bsz, t, LANES), F32)],
        scratch_shapes=[pltpu.VMEM((tm + 2 * halo, d), F32)],
        compiler_params=_params("parallel", "parallel"),
        name="conformer_router",
    )(x, x, x, norm_mix[None], w_pw1.astype(BF16), b_pw1[None], dww, dw_b[None], ln_g[None], ln_b[None],
      w_pw2.astype(BF16), b_pw2[None], norm_ffn[None], _router_weights(w_router))


def _route_kernel(lg_ref, pos_ref, aff_ref, tri_scr, *, t, cap):
    @pl.when(pl.program_id(0) == 0)
    def _():
        r = lax.broadcasted_iota(jnp.int32, (t, t), 0)
        cidx = lax.broadcasted_iota(jnp.int32, (t, t), 1)
        tri_scr[...] = jnp.where(r < cidx, 1.0, 0.0).astype(BF16)

    lt = lg_ref[0]
    e = jnp.exp(lt - jnp.max(lt, axis=0, keepdims=True))
    p = e / jnp.sum(e, axis=0, keepdims=True)
    bits = pltpu.bitcast(p, jnp.int32)
    count = lambda m: jnp.sum(jnp.where(m, 1.0, 0.0), axis=1, keepdims=True)
    thr = jnp.zeros((lt.shape[0], 1), jnp.int32)
    for bit in range(30, -1, -1):
        cand = thr | (1 << bit)
        thr = jnp.where(count(bits >= cand) >= cap, cand, thr)
    gt = bits > thr
    eq = bits == thr
    need = cap - count(gt)
    eq_before = _dot(jnp.where(eq, 1.0, 0.0).astype(BF16), tri_scr[...])
    sel = gt | (eq & (eq_before < need))
    slot = _dot(jnp.where(sel, 1.0, 0.0).astype(BF16), tri_scr[...])
    pos_ref[0] = jnp.where(sel, slot.astype(jnp.int32), -1)
    aff_ref[0] = jnp.where(sel, p, 0.0)


def _route(logits_t, cap):
    bsz, e, t = logits_t.shape
    blk = pl.BlockSpec((1, e, t), lambda b: (b, 0, 0))
    return pl.pallas_call(
        functools.partial(_route_kernel, t=t, cap=cap),
        grid=(bsz,),
        in_specs=[blk],
        out_specs=[blk, blk],
        out_shape=[jax.ShapeDtypeStruct((bsz, e, t), jnp.int32), jax.ShapeDtypeStruct((bsz, e, t), F32)],
        scratch_shapes=[pltpu.VMEM((t, t), BF16)],
        compiler_params=_params("arbitrary"),
        name="route",
    )(logits_t)


def _expert_kernel(hn_ref, pos_ref, aff_ref, wg_ref, wu_ref, wd_ref, y_ref, wgb, wub, wdb, *, cap):
    @pl.when(pl.program_id(1) == 0)
    def _():
        wgb[...] = wg_ref[0].astype(BF16)
        wub[...] = wu_ref[0].astype(BF16)
        wdb[...] = wd_ref[0].astype(BF16)

    pos = pos_ref[0, 0]
    t = pos.shape[-1]
    match = lax.broadcasted_iota(jnp.int32, (cap, t), 0) == pos
    xin = _dot(jnp.where(match, 1.0, 0.0).astype(BF16), hn_ref[0]).astype(BF16)
    aff = jnp.sum(jnp.where(match, aff_ref[0, 0], 0.0), axis=1, keepdims=True)
    hid = (_silu(_dot(xin, wgb[...])) * _dot(xin, wub[...])).astype(BF16)
    y_ref[0, 0] = (_dot(hid, wdb[...]) * aff).astype(BF16)


def _experts(hn, pos, aff, w_gate, w_up, w_down, cap):
    bsz, t, d = hn.shape
    e, _, ff = w_gate.shape
    row = pl.BlockSpec((1, 1, 1, t), lambda ei, b: (b, ei, 0, 0))
    return pl.pallas_call(
        functools.partial(_expert_kernel, cap=cap),
        grid=(e, bsz),
        in_specs=[pl.BlockSpec((1, t, d), lambda ei, b: (b, 0, 0)), row, row,
                  pl.BlockSpec((1, d, ff), lambda ei, b: (ei, 0, 0)),
                  pl.BlockSpec((1, d, ff), lambda ei, b: (ei, 0, 0)),
                  pl.BlockSpec((1, ff, d), lambda ei, b: (ei, 0, 0))],
        out_specs=pl.BlockSpec((1, 1, cap, d), lambda ei, b: (ei, b, 0, 0)),
        out_shape=jax.ShapeDtypeStruct((e, bsz, cap, d), BF16),
        scratch_shapes=[pltpu.VMEM((d, ff), BF16), pltpu.VMEM((d, ff), BF16), pltpu.VMEM((ff, d), BF16)],
        compiler_params=_params("arbitrary", "arbitrary"),
        name="experts",
    )(hn, pos[:, :, None, :], aff[:, :, None, :], w_gate, w_up, w_down)


def _combine_kernel(x_ref, post_ref, y_ref, fw_ref, o_ref, *, cap, final):
    e = y_ref.shape[0]
    d = y_ref.shape[-1]
    tm = x_ref.shape[1]
    post = post_ref[0]
    lane = lax.broadcasted_iota(jnp.int32, (tm, cap), 1)
    onehot = jnp.concatenate(
        [jnp.where(lane == post[:, ei:ei + 1], 1.0, 0.0).astype(BF16) for ei in range(e)], axis=1)
    out = x_ref[0] + _dot(onehot, y_ref[:, 0].reshape(e * cap, d))
    if final:
        out = _rms(out, fw_ref[...])
    o_ref[0] = out


def _combine(x, pos_t, y, final_w, cap, *, final, tm=512):
    bsz, t, d = x.shape
    e = y.shape[0]
    return pl.pallas_call(
        functools.partial(_combine_kernel, cap=cap, final=final),
        grid=(bsz, t // tm),
        in_specs=[pl.BlockSpec((1, tm, d), lambda b, tt: (b, tt, 0)),
                  pl.BlockSpec((1, tm, e), lambda b, tt: (b, tt, 0)),
                  pl.BlockSpec((e, 1, cap, d), lambda b, tt: (0, b, 0, 0)),
                  pl.BlockSpec((1, d), lambda b, tt: (0, 0))],
        out_specs=pl.BlockSpec((1, tm, d), lambda b, tt: (b, tt, 0)),
        out_shape=jax.ShapeDtypeStruct((bsz, t, d), F32),
        compiler_params=_params("parallel", "parallel"),
        name="combine",
    )(x, pos_t, y, final_w[None])


def _moe(x1, hn, logits, w_gate, w_up, w_down, final_w, *, final):
    t = x1.shape[1]
    cap = CAPACITY_FACTOR * t // N_EXPERTS
    pos, aff = _route(jnp.swapaxes(logits[:, :, :N_EXPERTS], 1, 2), cap)
    y = _experts(hn, pos, aff, w_gate, w_up, w_down, cap)
    return _combine(x1, jnp.swapaxes(pos, 1, 2), y, final_w, cap, final=final)


def kernel(x, norm_mix, norm_ffn, dn_w_in, dn_conv_w, dn_a_log, dn_dt_bias, dn_norm_w, dn_w_out, cf_w_pw1, cf_b_pw1, cf_dw_w, cf_dw_b, cf_ln_g, cf_ln_b, cf_w_pw2, cf_b_pw2, moe_w_router, moe_w_gate, moe_w_up, moe_w_down, final_norm):
    q, k, v, z, g = _gdn_pre(x, norm_mix[0], dn_w_in[0], dn_conv_w[0], dn_a_log[0], dn_dt_bias[0])
    og = _gdn_core(q, k, v, z, g, dn_norm_w[0])
    x1, hn, lg = _outproj_router(og, x, dn_w_out[0], norm_ffn[0], moe_w_router[0])
    x2 = _moe(x1, hn, lg, moe_w_gate[0], moe_w_up[0], moe_w_down[0], final_norm, final=False)
    x3, hn, lg = _conformer_router(x2, norm_mix[1], cf_w_pw1[0], cf_b_pw1[0], cf_dw_w[0], cf_dw_b[0],
                                   cf_ln_g[0], cf_ln_b[0], cf_w_pw2[0], cf_b_pw2[0], norm_ffn[1],
                                   moe_w_router[1])
    return _moe(x3, hn, lg, moe_w_gate[1], moe_w_up[1], moe_w_down[1], final_norm, final=True)
```

```python
import functools

import jax
import jax.numpy as jnp
from jax import lax
from jax.experimental import pallas as pl
from jax.experimental.pallas import tpu as pltpu

F32 = jnp.float32
BF16 = jnp.bfloat16
EPS = 1e-6

LANES = 128
SUBLANES = 8
HEADS = 8
HEAD_DIM = 128
DN_CONV = 5
CF_CONV = 31
N_EXPERTS = 16
CAPACITY_FACTOR = 2
GDN_CHUNK = 128
GDN_HEADS_PER_STEP = 2
CONV_BLOCK = 16
VMEM_LIMIT = 56 * 1024 * 1024


def _params(*sem):
    return pltpu.CompilerParams(dimension_semantics=sem, vmem_limit_bytes=VMEM_LIMIT)


def _sigmoid(x):
    return 1.0 / (1.0 + jnp.exp(-x))


def _silu(x):
    return x * _sigmoid(x)


def _softplus(x):
    return jnp.maximum(x, 0.0) + jnp.log1p(jnp.exp(-jnp.abs(x)))


def _rms(x, w):
    return x * lax.rsqrt(jnp.mean(x * x, axis=-1, keepdims=True) + EPS) * w


def _dot(a, b):
    return jnp.dot(a, b, preferred_element_type=F32)


def _bdot(spec, a, b):
    return jnp.einsum(spec, a, b, preferred_element_type=F32)


def _seg_cumsum(x, seg, reverse):
    n = x.shape[0]
    r = lax.broadcasted_iota(jnp.int32, (n, 1), 0) % seg
    s = 1
    while s < seg:
        if reverse:
            x = x + jnp.where(r < seg - s, pltpu.roll(x, n - s, axis=0), 0.0)
        else:
            x = x + jnp.where(r >= s, pltpu.roll(x, s, axis=0), 0.0)
        s *= 2
    return x


def _halo_specs(tm, halo, nt_total, d):
    per = tm // halo
    prev = pl.BlockSpec((1, halo, d), lambda b, t: (b, jnp.maximum(t * per - 1, 0), 0))
    cur = pl.BlockSpec((1, tm, d), lambda b, t: (b, t, 0))
    nxt = pl.BlockSpec((1, halo, d), lambda b, t: (b, jnp.minimum((t + 1) * per, nt_total - 1), 0))
    return prev, cur, nxt


def _ext_valid(tm, halo, nt):
    t = pl.program_id(1)
    rows = lax.broadcasted_iota(jnp.int32, (tm + 2 * halo, 1), 0)
    lo = jnp.where(t == 0, halo, 0)
    hi = jnp.where(t == nt - 1, tm + halo, tm + 2 * halo)
    return (rows >= lo) & (rows < hi)


def _to_token_major(dst_ref, x):
    rows = x.shape[0]
    for j in range(SUBLANES):
        dst_ref[pl.ds(j, rows, stride=SUBLANES), :] = x[:, j * LANES:(j + 1) * LANES]


def _from_token_major(src_ref, rows):
    return [src_ref[pl.ds(j, rows, stride=SUBLANES), :] for j in range(SUBLANES)]


def _depthwise_conv_blocks(src_ref, dst_ref, tap, n_taps, first_row, tm, post):
    rows = CONV_BLOCK * SUBLANES
    for i in range(tm // CONV_BLOCK):
        acc = jnp.zeros((CONV_BLOCK, SUBLANES, LANES), F32)
        for j in range(n_taps):
            start = (i * CONV_BLOCK + first_row + j) * SUBLANES
            seg = src_ref[pl.ds(start, rows), :].reshape(CONV_BLOCK, SUBLANES, LANES)
            acc = acc + seg * tap(j)[None]
        dst_ref[pl.ds(i * rows, rows), :] = post(acc.reshape(rows, LANES))


def _gdn_pre_kernel(xp_ref, xc_ref, xn_ref, nw_ref, wqkvz_ref, wg_ref, cw_ref, alog_ref, dtb_ref,
                    q_ref, k_ref, v_ref, z_ref, g_ref, grow_ref, ext_ref, act_ref,
                    *, tm, nt, halo, chunk):
    width = HEADS * HEAD_DIM
    x_ext = jnp.concatenate([xp_ref[0], xc_ref[0], xn_ref[0]], axis=0)
    hn = _rms(x_ext, nw_ref[...]).astype(BF16)
    proj = _dot(hn, wqkvz_ref[...])
    valid = _ext_valid(tm, halo, nt)
    for h in range(HEADS):
        z_ref[0, h] = proj[halo:halo + tm, 3 * width + h * HEAD_DIM: 3 * width + (h + 1) * HEAD_DIM]

    l2 = lambda a: a * lax.rsqrt(jnp.sum(a * a, axis=-1, keepdims=True) + EPS)
    posts = (lambda a: l2(a) * (HEAD_DIM ** -0.5), l2, lambda a: a)
    for grp, out_ref in enumerate((q_ref, k_ref, v_ref)):
        _to_token_major(ext_ref, jnp.where(valid, proj[:, grp * width:(grp + 1) * width], 0.0))
        _depthwise_conv_blocks(ext_ref, act_ref, lambda j: cw_ref[j, grp], DN_CONV,
                               halo - DN_CONV // 2, tm, _silu)
        for h, tile in enumerate(_from_token_major(act_ref, tm)):
            out_ref[0, h] = posts[grp](tile).astype(BF16)

    graw = _dot(hn[halo:halo + tm], wg_ref[...])
    lane = lax.broadcasted_iota(jnp.int32, (1, LANES), 1)
    beta = _sigmoid(graw)
    logg = -jnp.exp(alog_ref[...]) * _softplus(graw + dtb_ref[...])
    cum_f = _seg_cumsum(logg, chunk, reverse=False)
    cum_b = _seg_cumsum(logg, chunk, reverse=True)
    is_beta = ((lane % (2 * HEADS)) < HEADS) & (lane < 4 * HEADS)
    is_f = (lane >= HEADS) & (lane < 2 * HEADS)
    is_b = (lane >= 3 * HEADS) & (lane < 4 * HEADS)
    g = jnp.where(is_beta, beta, jnp.where(is_f, cum_f, jnp.where(is_b, cum_b, 0.0)))
    g_ref[0] = g
    grow_ref[0] = g.T[:4 * HEADS]


def _gdn_pre(x, norm_w, w_in, conv_w, a_log, dt_bias, *, tm=256, halo=8):
    bsz, t, d = x.shape
    width = HEADS * HEAD_DIM
    nt = t // tm
    wqkvz = w_in[:, :4 * width].astype(BF16)
    wg = jnp.pad(w_in[:, 4 * width:], ((0, 0), (0, LANES - 4 * HEADS))).astype(BF16)
    cw = jnp.pad(conv_w, ((0, 8 - DN_CONV), (0, 0))).reshape(8, 3, HEADS, HEAD_DIM)
    zeros = jnp.zeros((HEADS,), F32)
    alog = jnp.pad(jnp.concatenate([zeros, a_log[0], zeros, a_log[1]]), (0, LANES - 4 * HEADS))[None]
    dtb = jnp.pad(jnp.concatenate([zeros, dt_bias[0], zeros, dt_bias[1]]), (0, LANES - 4 * HEADS))[None]
    prev, cur, nxt = _halo_specs(tm, halo, t // halo, d)
    full = lambda shape: pl.BlockSpec(shape, lambda b, tt: (0,) * len(shape))
    heads = pl.BlockSpec((1, HEADS, tm, HEAD_DIM), lambda b, tt: (b, 0, tt, 0))
    hshape = (bsz, HEADS, t, HEAD_DIM)
    return pl.pallas_call(
        functools.partial(_gdn_pre_kernel, tm=tm, nt=nt, halo=halo, chunk=GDN_CHUNK),
        grid=(bsz, nt),
        in_specs=[prev, cur, nxt, full((1, d)), full((d, 4 * width)), full((d, LANES)),
                  full((8, 3, HEADS, HEAD_DIM)), full((1, LANES)), full((1, LANES))],
        out_specs=[heads, heads, heads, heads,
                   pl.BlockSpec((1, tm, LANES), lambda b, tt: (b, tt, 0)),
                   pl.BlockSpec((1, 4 * HEADS, tm), lambda b, tt: (b, 0, tt))],
        out_shape=[jax.ShapeDtypeStruct(hshape, BF16)] * 3
        + [jax.ShapeDtypeStruct(hshape, F32), jax.ShapeDtypeStruct((bsz, t, LANES), F32),
           jax.ShapeDtypeStruct((bsz, 4 * HEADS, t), F32)],
        scratch_shapes=[pltpu.VMEM(((tm + 2 * halo) * SUBLANES, LANES), F32),
                        pltpu.VMEM((tm * SUBLANES, LANES), F32)],
        compiler_params=_params("parallel", "parallel"),
        name="gdn_pre",
    )(x, x, x, norm_w[None], wqkvz, wg, cw, alog, dtb)


def _unit_tri_inverse(a, c):
    ii = lax.broadcasted_iota(jnp.int32, (c, c), 0)
    jj = lax.broadcasted_iota(jnp.int32, (c, c), 1)
    eye = (ii == jj).astype(F32)
    x = eye - jnp.where((ii // 2) == (jj // 2), a, 0.0)
    s = 2
    while s < c:
        off = ((ii // (2 * s)) == (jj // (2 * s))) & ((ii // s) != (jj // s))
        a_l = jnp.where(off, a, 0.0).astype(BF16)
        y = _bdot('nij,njk->nik', x.astype(BF16), a_l)
        x = x - _bdot('nij,njk->nik', y.astype(BF16), x.astype(BF16))
        s *= 2
    return x


def _gdn_core_kernel(q_ref, k_ref, v_ref, z_ref, g_ref, grow_ref, nw_ref, o_ref,
                     mf_scr, mb_scr, nf_scr, nb_scr, qpf_scr, qpb_scr, sf_scr, sb_scr, oacc_scr,
                     *, t, c, hb):
    n = t // c
    d = HEAD_DIM
    lane = lax.broadcasted_iota(jnp.int32, (1, LANES), 1)
    ii = lax.broadcasted_iota(jnp.int32, (c, c), 0)
    jj = lax.broadcasted_iota(jnp.int32, (c, c), 1)
    eye_d = (lax.broadcasted_iota(jnp.int32, (d, d), 0) == lax.broadcasted_iota(jnp.int32, (d, d), 1)).astype(F32)
    m_scr, n_scr, qp_scr, s_scr = (mf_scr, mb_scr), (nf_scr, nb_scr), (qpf_scr, qpb_scr), (sf_scr, sb_scr)

    def precompute(hh, carry):
        h = pl.program_id(1) * hb + hh
        gt = g_ref[0]
        kb = k_ref[0, hh].reshape(n, c, d)
        qb = q_ref[0, hh].reshape(n, c, d)
        kf = kb.astype(F32)
        qf = qb.astype(F32)
        vf = v_ref[0, hh].astype(F32).reshape(n, c, d)
        kk = _bdot('ncd,nsd->ncs', kb, kb)
        qk = _bdot('ncd,nsd->ncs', qb, kb)
        for di in range(2):
            col = lambda idx: jnp.sum(jnp.where(lane == idx * HEADS + h, gt, 0.0), axis=-1,
                                      keepdims=True).reshape(n, c, 1)
            beta = col(2 * di)
            gcol = col(2 * di + 1)
            grow = grow_ref[0, 2 * di + 1, hh]
            incl = (ii >= jj) if di == 0 else (ii <= jj)
            strict = (ii > jj) if di == 0 else (ii < jj)
            decay = jnp.where(incl, jnp.exp(jnp.where(incl, gcol - grow, 0.0)), 0.0)
            a = jnp.where(strict, beta * kk * decay, 0.0)
            tinv = _unit_tri_inverse(a, c)
            egc = jnp.exp(gcol)
            rhs = jnp.concatenate([vf * beta, kf * (beta * egc)], axis=-1).astype(BF16)
            uw = _bdot('ncs,nsd->ncd', tinv.astype(BF16), rhs).astype(BF16)
            iw = _bdot('ncs,nsd->ncd', (qk * decay).astype(BF16), uw)
            qp = qf * egc - iw[..., d:]
            glast = gcol[:, c - 1:c, :] if di == 0 else gcol[:, 0:1, :]
            kt = (kf * jnp.exp(glast - gcol)).astype(BF16)
            ktuw = _bdot('ncd,nce->nde', kt, uw)
            m_scr[di][hh] = (jnp.exp(glast) * eye_d - ktuw[..., d:]).astype(BF16)
            n_scr[di][hh] = ktuw[..., :d]
            qp_scr[di][hh] = qp.reshape(t, d).astype(BF16)
            if di == 0:
                oacc_scr[hh] = iw[..., :d].reshape(t, d)
            else:
                oacc_scr[hh] += iw[..., :d].reshape(t, d)
        return carry

    lax.fori_loop(0, hb, precompute, 0)

    for hh in range(hb):
        sf_scr[hh, 0] = jnp.zeros((d, d), BF16)
        sb_scr[hh, n - 1] = jnp.zeros((d, d), BF16)

    def scan(i, carry):
        nb = n - 1 - i
        ins = [(mf_scr[hh, i], sf_scr[hh, i], nf_scr[hh, i], mb_scr[hh, nb], sb_scr[hh, nb], nb_scr[hh, nb])
               for hh in range(hb)]
        outs = [((_dot(mf, sf) + nf).astype(BF16), (_dot(mb, sb) + nbv).astype(BF16))
                for mf, sf, nf, mb, sb, nbv in ins]
        for hh, (of, ob) in enumerate(outs):
            sf_scr[hh, i + 1] = of
            sb_scr[hh, nb - 1] = ob
        return carry

    lax.fori_loop(0, n - 1, scan, 0)

    def finish(hh, carry):
        o = oacc_scr[hh]
        for di in range(2):
            qp = qp_scr[di][hh].reshape(n, c, d)
            o = o + _bdot('ncd,nde->nce', qp, s_scr[di][hh]).reshape(t, d)
        o_ref[0, hh] = (_rms(o, nw_ref[...]) * _silu(z_ref[0, hh])).astype(BF16)
        return carry

    lax.fori_loop(0, hb, finish, 0)


def _gdn_core(q, k, v, z, g, grow, norm_w):
    bsz, _, t, d = q.shape
    c, hb = GDN_CHUNK, GDN_HEADS_PER_STEP
    n = t // c
    grow = grow.reshape(bsz, 4, HEADS, n, 1, c)
    head = pl.BlockSpec((1, hb, t, d), lambda b, h: (b, h, 0, 0))
    mats = lambda dt: pltpu.VMEM((hb, n, d, d), dt)
    return pl.pallas_call(
        functools.partial(_gdn_core_kernel, t=t, c=c, hb=hb),
        grid=(bsz, HEADS // hb),
        in_specs=[head, head, head, head,
                  pl.BlockSpec((1, t, LANES), lambda b, h: (b, 0, 0)),
                  pl.BlockSpec((1, 4, hb, n, 1, c), lambda b, h: (b, 0, h, 0, 0, 0)),
                  pl.BlockSpec((1, d), lambda b, h: (0, 0))],
        out_specs=head,
        out_shape=jax.ShapeDtypeStruct((bsz, HEADS, t, d), BF16),
        scratch_shapes=[mats(BF16), mats(BF16), mats(F32), mats(F32),
                        pltpu.VMEM((hb, t, d), BF16), pltpu.VMEM((hb, t, d), BF16),
                        mats(BF16), mats(BF16), pltpu.VMEM((hb, t, d), F32)],
        compiler_params=_params("parallel", "parallel"),
        name="gdn_core",
    )(q, k, v, z, g, grow, norm_w[None])


def _router_epilogue(x1, nffn_ref, wr_ref, hn_ref, lgt_ref):
    hn = _rms(x1, nffn_ref[...])
    hi = hn.astype(BF16)
    lo = (hn - hi.astype(F32)).astype(BF16)
    hn_ref[0] = hi
    lg = _dot(hi, wr_ref[0]) + _dot(hi, wr_ref[1]) + _dot(lo, wr_ref[0])
    lgt_ref[0] = lg.T[:N_EXPERTS]


def _router_weights(w_router):
    wp = jnp.pad(w_router, ((0, 0), (0, LANES - N_EXPERTS)))
    hi = wp.astype(BF16)
    lo = (wp - hi.astype(F32)).astype(BF16)
    return jnp.stack([hi, lo])


def _outproj_kernel(og_ref, x_ref, wout_ref, nffn_ref, wr_ref, x1_ref, hn_ref, lgt_ref):
    og = jnp.concatenate([og_ref[0, h] for h in range(HEADS)], axis=-1)
    x1 = x_ref[0] + _dot(og, wout_ref[...])
    x1_ref[0] = x1
    _router_epilogue(x1, nffn_ref, wr_ref, hn_ref, lgt_ref)


def _router_out(bsz, t, d, tm):
    tile = lambda w: pl.BlockSpec((1, tm, w), lambda b, tt: (b, tt, 0))
    specs = [tile(d), tile(d), pl.BlockSpec((1, N_EXPERTS, tm), lambda b, tt: (b, 0, tt))]
    shapes = [jax.ShapeDtypeStruct((bsz, t, d), F32), jax.ShapeDtypeStruct((bsz, t, d), BF16),
              jax.ShapeDtypeStruct((bsz, N_EXPERTS, t), F32)]
    return specs, shapes


def _outproj_router(og, x, w_out, norm_ffn, w_router, *, tm=512):
    bsz, t, d = x.shape
    full = lambda shape: pl.BlockSpec(shape, lambda b, tt: (0,) * len(shape))
    out_specs, out_shape = _router_out(bsz, t, d, tm)
    return pl.pallas_call(
        _outproj_kernel,
        grid=(bsz, t // tm),
        in_specs=[pl.BlockSpec((1, HEADS, tm, HEAD_DIM), lambda b, tt: (b, 0, tt, 0)),
                  pl.BlockSpec((1, tm, d), lambda b, tt: (b, tt, 0)),
                  full(w_out.shape), full((1, d)), full((2, d, LANES))],
        out_specs=out_specs,
        out_shape=out_shape,
        compiler_params=_params("parallel", "parallel"),
        name="outproj_router",
    )(og, x, w_out.astype(BF16), norm_ffn[None], _router_weights(w_router))


def _conformer_kernel(xp_ref, xc_ref, xn_ref, nmix_ref, w1_ref, b1_ref, dww_ref, dwb_ref, lng_ref, lnb_ref,
                      w2_ref, b2_ref, nffn_ref, wr_ref, x1_ref, hn_ref, lgt_ref, u_scr, c_scr,
                      *, tm, nt, halo):
    d = xc_ref.shape[-1]
    x_ext = jnp.concatenate([xp_ref[0], xc_ref[0], xn_ref[0]], axis=0)
    hn = _rms(x_ext, nmix_ref[...]).astype(BF16)
    pg = _dot(hn, w1_ref[...]) + b1_ref[...]
    u = pg[:, :d] * _sigmoid(pg[:, d:])
    _to_token_major(u_scr, jnp.where(_ext_valid(tm, halo, nt), u, 0.0))
    _depthwise_conv_blocks(u_scr, c_scr, lambda j: dww_ref[j], CF_CONV, halo - CF_CONV // 2, tm, lambda a: a)
    conv = jnp.concatenate(_from_token_major(c_scr, tm), axis=-1) + dwb_ref[...]
    xc = conv - jnp.mean(conv, axis=-1, keepdims=True)
    y = xc * lax.rsqrt(jnp.mean(xc * xc, axis=-1, keepdims=True) + EPS) * lng_ref[...] + lnb_ref[...]
    mix = _dot(_silu(y).astype(BF16), w2_ref[...]) + b2_ref[...]
    x1 = xc_ref[0] + mix
    x1_ref[0] = x1
    _router_epilogue(x1, nffn_ref, wr_ref, hn_ref, lgt_ref)


def _conformer_router(x, norm_mix, w_pw1, b_pw1, dw_w, dw_b, ln_g, ln_b, w_pw2, b_pw2, norm_ffn, w_router,
                      *, tm=256, halo=16):
    bsz, t, d = x.shape
    nt = t // tm
    prev, cur, nxt = _halo_specs(tm, halo, t // halo, d)
    full = lambda shape: pl.BlockSpec(shape, lambda b, tt: (0,) * len(shape))
    dww = jnp.pad(dw_w, ((0, 32 - CF_CONV), (0, 0))).reshape(32, SUBLANES, LANES)
    out_specs, out_shape = _router_out(bsz, t, d, tm)
    return pl.pallas_call(
        functools.partial(_conformer_kernel, tm=tm, nt=nt, halo=halo),
        grid=(bsz, nt),
        in_specs=[prev, cur, nxt, full((1, d)), full((d, 2 * d)), full((1, 2 * d)),
                  full((32, SUBLANES, LANES)), full((1, d)), full((1, d)), full((1, d)), full((d, d)),
                  full((1, d)), full((1, d)), full((2, d, LANES))],
        out_specs=out_specs,
        out_shape=out_shape,
        scratch_shapes=[pltpu.VMEM(((tm + 2 * halo) * SUBLANES, LANES), F32),
                        pltpu.VMEM((tm * SUBLANES, LANES), F32)],
        compiler_params=_params("parallel", "parallel"),
        name="conformer_router",
    )(x, x, x, norm_mix[None], w_pw1.astype(BF16), b_pw1[None], dww, dw_b[None], ln_g[None], ln_b[None],
      w_pw2.astype(BF16), b_pw2[None], norm_ffn[None], _router_weights(w_router))


def _route_kernel(lg_ref, pos_ref, post_ref, aff_ref, tri_scr, *, t, cap):
    @pl.when(pl.program_id(0) == 0)
    def _():
        r = lax.broadcasted_iota(jnp.int32, (t, t), 0)
        cidx = lax.broadcasted_iota(jnp.int32, (t, t), 1)
        tri_scr[...] = jnp.where(r < cidx, 1.0, 0.0).astype(BF16)

    lt = lg_ref[0]
    ne = lt.shape[0]
    e = jnp.exp(lt - jnp.max(lt, axis=0, keepdims=True))
    p = e / jnp.sum(e, axis=0, keepdims=True)
    bits = pltpu.bitcast(p, jnp.int32)
    count = lambda m: jnp.sum(jnp.where(m, 1.0, 0.0), axis=1, keepdims=True)
    thr = jnp.zeros((ne, 1), jnp.int32)
    for bit in range(30, -1, -1):
        cand = thr | (1 << bit)
        thr = jnp.where(count(bits >= cand) >= cap, cand, thr)
    gt = bits > thr
    eq = bits == thr
    need = cap - count(gt)
    eq_before = _dot(jnp.where(eq, 1.0, 0.0).astype(BF16), tri_scr[...])
    sel = gt | (eq & (eq_before < need))
    slot = _dot(jnp.where(sel, 1.0, 0.0).astype(BF16), tri_scr[...])
    pos = jnp.where(sel, slot.astype(jnp.int32), -1)
    pos_ref[0] = pos
    post_ref[0] = jnp.concatenate([pos, jnp.full((LANES - ne, t), -1, jnp.int32)], axis=0).T
    aff_ref[0] = jnp.where(sel, p, 0.0)


def _route(logits_t, cap):
    bsz, e, t = logits_t.shape
    blk = pl.BlockSpec((1, e, t), lambda b: (b, 0, 0))
    return pl.pallas_call(
        functools.partial(_route_kernel, t=t, cap=cap),
        grid=(bsz,),
        in_specs=[blk],
        out_specs=[blk, pl.BlockSpec((1, t, LANES), lambda b: (b, 0, 0)), blk],
        out_shape=[jax.ShapeDtypeStruct((bsz, e, t), jnp.int32),
                   jax.ShapeDtypeStruct((bsz, t, LANES), jnp.int32),
                   jax.ShapeDtypeStruct((bsz, e, t), F32)],
        scratch_shapes=[pltpu.VMEM((t, t), BF16)],
        compiler_params=_params("arbitrary"),
        name="route",
    )(logits_t)


def _expert_kernel(hn_ref, pos_ref, aff_ref, wg_ref, wu_ref, wd_ref, y_ref, wgb, wub, wdb, *, cap):
    @pl.when(pl.program_id(1) == 0)
    def _():
        wgb[...] = wg_ref[0].astype(BF16)
        wub[...] = wu_ref[0].astype(BF16)
        wdb[...] = wd_ref[0].astype(BF16)

    pos = pos_ref[0, 0]
    t = pos.shape[-1]
    match = lax.broadcasted_iota(jnp.int32, (cap, t), 0) == pos
    xin = _dot(jnp.where(match, 1.0, 0.0).astype(BF16), hn_ref[0]).astype(BF16)
    aff = jnp.sum(jnp.where(match, aff_ref[0, 0], 0.0), axis=1, keepdims=True)
    hid = (_silu(_dot(xin, wgb[...])) * _dot(xin, wub[...])).astype(BF16)
    y_ref[0, 0] = (_dot(hid, wdb[...]) * aff).astype(BF16)


def _experts(hn, pos, aff, w_gate, w_up, w_down, cap):
    bsz, t, d = hn.shape
    e, _, ff = w_gate.shape
    row = pl.BlockSpec((1, 1, 1, t), lambda ei, b: (b, ei, 0, 0))
    return pl.pallas_call(
        functools.partial(_expert_kernel, cap=cap),
        grid=(e, bsz),
        in_specs=[pl.BlockSpec((1, t, d), lambda ei, b: (b, 0, 0)), row, row,
                  pl.BlockSpec((1, d, ff), lambda ei, b: (ei, 0, 0)),
                  pl.BlockSpec((1, d, ff), lambda ei, b: (ei, 0, 0)),
                  pl.BlockSpec((1, ff, d), lambda ei, b: (ei, 0, 0))],
        out_specs=pl.BlockSpec((1, 1, cap, d), lambda ei, b: (ei, b, 0, 0)),
        out_shape=jax.ShapeDtypeStruct((e, bsz, cap, d), BF16),
        scratch_shapes=[pltpu.VMEM((d, ff), BF16), pltpu.VMEM((d, ff), BF16), pltpu.VMEM((ff, d), BF16)],
        compiler_params=_params("arbitrary", "arbitrary"),
        name="experts",
    )(hn, pos[:, :, None, :], aff[:, :, None, :], w_gate, w_up, w_down)


def _combine_kernel(x_ref, post_ref, y_ref, fw_ref, o_ref, *, cap, final):
    e = y_ref.shape[0]
    d = y_ref.shape[-1]
    tm = x_ref.shape[1]
    post = post_ref[0]
    lane = lax.broadcasted_iota(jnp.int32, (tm, cap), 1)
    onehot = jnp.concatenate(
        [jnp.where(lane == post[:, ei:ei + 1], 1.0, 0.0).astype(BF16) for ei in range(e)], axis=1)
    out = x_ref[0] + _dot(onehot, y_ref[:, 0].reshape(e * cap, d))
    if final:
        out = _rms(out, fw_ref[...])
    o_ref[0] = out


def _combine(x, pos_t, y, final_w, cap, *, final, tm=512):
    bsz, t, d = x.shape
    e = y.shape[0]
    return pl.pallas_call(
        functools.partial(_combine_kernel, cap=cap, final=final),
        grid=(bsz, t // tm),
        in_specs=[pl.BlockSpec((1, tm, d), lambda b, tt: (b, tt, 0)),
                  pl.BlockSpec((1, tm, LANES), lambda b, tt: (b, tt, 0)),
                  pl.BlockSpec((e, 1, cap, d), lambda b, tt: (0, b, 0, 0)),
                  pl.BlockSpec((1, d), lambda b, tt: (0, 0))],
        out_specs=pl.BlockSpec((1, tm, d), lambda b, tt: (b, tt, 0)),
        out_shape=jax.ShapeDtypeStruct((bsz, t, d), F32),
        compiler_params=_params("parallel", "parallel"),
        name="combine",
    )(x, pos_t, y, final_w[None])


def _moe(x1, hn, logits_t, w_gate, w_up, w_down, final_w, *, final):
    t = x1.shape[1]
    cap = CAPACITY_FACTOR * t // N_EXPERTS
    pos, pos_t, aff = _route(logits_t, cap)
    y = _experts(hn, pos, aff, w_gate, w_up, w_down, cap)
    return _combine(x1, pos_t, y, final_w, cap, final=final)


def kernel(x, norm_mix, norm_ffn, dn_w_in, dn_conv_w, dn_a_log, dn_dt_bias, dn_norm_w, dn_w_out, cf_w_pw1, cf_b_pw1, cf_dw_w, cf_dw_b, cf_ln_g, cf_ln_b, cf_w_pw2, cf_b_pw2, moe_w_router, moe_w_gate, moe_w_up, moe_w_down, final_norm):
    q, k, v, z, g, grow = _gdn_pre(x, norm_mix[0], dn_w_in[0], dn_conv_w[0], dn_a_log[0], dn_dt_bias[0])
    og = _gdn_core(q, k, v, z, g, grow, dn_norm_w[0])
    x1, hn, lgt = _outproj_router(og, x, dn_w_out[0], norm_ffn[0], moe_w_router[0])
    x2 = _moe(x1, hn, lgt, moe_w_gate[0], moe_w_up[0], moe_w_down[0], final_norm, final=False)
    x3, hn, lgt = _conformer_router(x2, norm_mix[1], cf_w_pw1[0], cf_b_pw1[0], cf_dw_w[0], cf_dw_b[0],
                                    cf_ln_g[0], cf_ln_b[0], cf_w_pw2[0], cf_b_pw2[0], norm_ffn[1],
                                    moe_w_router[1])
    return _moe(x3, hn, lgt, moe_w_gate[1], moe_w_up[1], moe_w_down[1], final_norm, final=True)
```

```python
import functools

import jax
import jax.numpy as jnp
from jax import lax
from jax.experimental import pallas as pl
from jax.experimental.pallas import tpu as pltpu

F32 = jnp.float32
BF16 = jnp.bfloat16
EPS = 1e-6

LANES = 128
SUBLANES = 8
HEADS = 8
HEAD_DIM = 128
DN_CONV = 5
CF_CONV = 31
N_EXPERTS = 16
CAPACITY_FACTOR = 2
GDN_CHUNK = 128
GDN_HEADS_PER_STEP = 2
CONV_BLOCK = 16
VMEM_LIMIT = 56 * 1024 * 1024


def _params(*sem):
    return pltpu.CompilerParams(dimension_semantics=sem, vmem_limit_bytes=VMEM_LIMIT)


def _sigmoid(x):
    return 1.0 / (1.0 + jnp.exp(-x))


def _silu(x):
    return x * _sigmoid(x)


def _softplus(x):
    return jnp.maximum(x, 0.0) + jnp.log1p(jnp.exp(-jnp.abs(x)))


def _rms(x, w):
    return x * lax.rsqrt(jnp.mean(x * x, axis=-1, keepdims=True) + EPS) * w


def _dot(a, b):
    return jnp.dot(a, b, preferred_element_type=F32)


def _bdot(spec, a, b):
    return jnp.einsum(spec, a, b, preferred_element_type=F32)


def _seg_cumsum(x, seg, reverse):
    n = x.shape[0]
    r = lax.broadcasted_iota(jnp.int32, (n, 1), 0) % seg
    s = 1
    while s < seg:
        if reverse:
            x = x + jnp.where(r < seg - s, pltpu.roll(x, n - s, axis=0), 0.0)
        else:
            x = x + jnp.where(r >= s, pltpu.roll(x, s, axis=0), 0.0)
        s *= 2
    return x


def _halo_specs(tm, halo, nt_total, d):
    per = tm // halo
    prev = pl.BlockSpec((1, halo, d), lambda b, t: (b, jnp.maximum(t * per - 1, 0), 0))
    cur = pl.BlockSpec((1, tm, d), lambda b, t: (b, t, 0))
    nxt = pl.BlockSpec((1, halo, d), lambda b, t: (b, jnp.minimum((t + 1) * per, nt_total - 1), 0))
    return prev, cur, nxt


def _ext_valid(tm, halo, nt):
    t = pl.program_id(1)
    rows = lax.broadcasted_iota(jnp.int32, (tm + 2 * halo, 1), 0)
    lo = jnp.where(t == 0, halo, 0)
    hi = jnp.where(t == nt - 1, tm + halo, tm + 2 * halo)
    return (rows >= lo) & (rows < hi)


def _to_token_major(dst_ref, x):
    rows = x.shape[0]
    for j in range(SUBLANES):
        dst_ref[pl.ds(j, rows, stride=SUBLANES), :] = x[:, j * LANES:(j + 1) * LANES]


def _from_token_major(src_ref, rows):
    return [src_ref[pl.ds(j, rows, stride=SUBLANES), :] for j in range(SUBLANES)]


def _depthwise_conv_blocks(src_ref, dst_ref, tap, n_taps, first_row, tm, post):
    rows = CONV_BLOCK * SUBLANES
    for i in range(tm // CONV_BLOCK):
        acc = jnp.zeros((CONV_BLOCK, SUBLANES, LANES), F32)
        for j in range(n_taps):
            start = (i * CONV_BLOCK + first_row + j) * SUBLANES
            seg = src_ref[pl.ds(start, rows), :].reshape(CONV_BLOCK, SUBLANES, LANES)
            acc = acc + seg * tap(j)[None]
        dst_ref[pl.ds(i * rows, rows), :] = post(acc.reshape(rows, LANES))


def _gdn_pre_kernel(xp_ref, xc_ref, xn_ref, nw_ref, wqkvz_ref, wg_ref, cw_ref, alog_ref, dtb_ref,
                    q_ref, k_ref, v_ref, z_ref, g_ref, grow_ref, ext_ref, act_ref,
                    *, tm, nt, halo, chunk):
    width = HEADS * HEAD_DIM
    x_ext = jnp.concatenate([xp_ref[0], xc_ref[0], xn_ref[0]], axis=0)
    hn = _rms(x_ext, nw_ref[...]).astype(BF16)
    proj = _dot(hn, wqkvz_ref[...])
    valid = _ext_valid(tm, halo, nt)
    for h in range(HEADS):
        z_ref[0, h] = proj[halo:halo + tm, 3 * width + h * HEAD_DIM: 3 * width + (h + 1) * HEAD_DIM]

    l2 = lambda a: a * lax.rsqrt(jnp.sum(a * a, axis=-1, keepdims=True) + EPS)
    posts = (lambda a: l2(a) * (HEAD_DIM ** -0.5), l2, lambda a: a)
    for grp, out_ref in enumerate((q_ref, k_ref, v_ref)):
        _to_token_major(ext_ref, jnp.where(valid, proj[:, grp * width:(grp + 1) * width], 0.0))
        _depthwise_conv_blocks(ext_ref, act_ref, lambda j: cw_ref[j, grp], DN_CONV,
                               halo - DN_CONV // 2, tm, _silu)
        for h, tile in enumerate(_from_token_major(act_ref, tm)):
            out_ref[0, h] = posts[grp](tile).astype(BF16)

    graw = _dot(hn[halo:halo + tm], wg_ref[...])
    lane = lax.broadcasted_iota(jnp.int32, (1, LANES), 1)
    beta = _sigmoid(graw)
    logg = -jnp.exp(alog_ref[...]) * _softplus(graw + dtb_ref[...])
    cum_f = _seg_cumsum(logg, chunk, reverse=False)
    cum_b = _seg_cumsum(logg, chunk, reverse=True)
    is_beta = ((lane % (2 * HEADS)) < HEADS) & (lane < 4 * HEADS)
    is_f = (lane >= HEADS) & (lane < 2 * HEADS)
    is_b = (lane >= 3 * HEADS) & (lane < 4 * HEADS)
    g = jnp.where(is_beta, beta, jnp.where(is_f, cum_f, jnp.where(is_b, cum_b, 0.0)))
    g_ref[0] = g
    grow_ref[0] = g.T[:4 * HEADS]


def _gdn_pre(x, norm_w, w_in, conv_w, a_log, dt_bias, *, tm=256, halo=8):
    bsz, t, d = x.shape
    width = HEADS * HEAD_DIM
    nt = t // tm
    wqkvz = w_in[:, :4 * width].astype(BF16)
    wg = jnp.pad(w_in[:, 4 * width:], ((0, 0), (0, LANES - 4 * HEADS))).astype(BF16)
    cw = jnp.pad(conv_w, ((0, 8 - DN_CONV), (0, 0))).reshape(8, 3, HEADS, HEAD_DIM)
    zeros = jnp.zeros((HEADS,), F32)
    alog = jnp.pad(jnp.concatenate([zeros, a_log[0], zeros, a_log[1]]), (0, LANES - 4 * HEADS))[None]
    dtb = jnp.pad(jnp.concatenate([zeros, dt_bias[0], zeros, dt_bias[1]]), (0, LANES - 4 * HEADS))[None]
    prev, cur, nxt = _halo_specs(tm, halo, t // halo, d)
    full = lambda shape: pl.BlockSpec(shape, lambda b, tt: (0,) * len(shape))
    heads = pl.BlockSpec((1, HEADS, tm, HEAD_DIM), lambda b, tt: (b, 0, tt, 0))
    hshape = (bsz, HEADS, t, HEAD_DIM)
    return pl.pallas_call(
        functools.partial(_gdn_pre_kernel, tm=tm, nt=nt, halo=halo, chunk=GDN_CHUNK),
        grid=(bsz, nt),
        in_specs=[prev, cur, nxt, full((1, d)), full((d, 4 * width)), full((d, LANES)),
                  full((8, 3, HEADS, HEAD_DIM)), full((1, LANES)), full((1, LANES))],
        out_specs=[heads, heads, heads, heads,
                   pl.BlockSpec((1, tm, LANES), lambda b, tt: (b, tt, 0)),
                   pl.BlockSpec((1, 4 * HEADS, tm), lambda b, tt: (b, 0, tt))],
        out_shape=[jax.ShapeDtypeStruct(hshape, BF16)] * 3
        + [jax.ShapeDtypeStruct(hshape, F32), jax.ShapeDtypeStruct((bsz, t, LANES), F32),
           jax.ShapeDtypeStruct((bsz, 4 * HEADS, t), F32)],
        scratch_shapes=[pltpu.VMEM(((tm + 2 * halo) * SUBLANES, LANES), F32),
                        pltpu.VMEM((tm * SUBLANES, LANES), F32)],
        compiler_params=_params("parallel", "parallel"),
        name="gdn_pre",
    )(x, x, x, norm_w[None], wqkvz, wg, cw, alog, dtb)


def _unit_tri_inverse(a, c):
    ii = lax.broadcasted_iota(jnp.int32, (c, c), 0)
    jj = lax.broadcasted_iota(jnp.int32, (c, c), 1)
    eye = (ii == jj).astype(F32)
    x = eye - jnp.where((ii // 2) == (jj // 2), a, 0.0)
    s = 2
    while s < c:
        off = ((ii // (2 * s)) == (jj // (2 * s))) & ((ii // s) != (jj // s))
        a_l = jnp.where(off, a, 0.0).astype(BF16)
        y = _bdot('nij,njk->nik', x.astype(BF16), a_l)
        x = x - _bdot('nij,njk->nik', y.astype(BF16), x.astype(BF16))
        s *= 2
    return x


def _gdn_core_kernel(q_ref, k_ref, v_ref, z_ref, g_ref, grow_ref, nw_ref, o_ref,
                     mf_scr, mb_scr, nf_scr, nb_scr, qpf_scr, qpb_scr, sf_scr, sb_scr, oacc_scr,
                     *, t, c, hb):
    n = t // c
    d = HEAD_DIM
    lane = lax.broadcasted_iota(jnp.int32, (1, LANES), 1)
    ii = lax.broadcasted_iota(jnp.int32, (c, c), 0)
    jj = lax.broadcasted_iota(jnp.int32, (c, c), 1)
    eye_d = (lax.broadcasted_iota(jnp.int32, (d, d), 0) == lax.broadcasted_iota(jnp.int32, (d, d), 1)).astype(F32)
    m_scr, n_scr, qp_scr = (mf_scr, mb_scr), (nf_scr, nb_scr), (qpf_scr, qpb_scr)

    def precompute(hh, carry):
        h = pl.program_id(1) * hb + hh
        gt = g_ref[0]
        kb = k_ref[0, hh].reshape(n, c, d)
        qb = q_ref[0, hh].reshape(n, c, d)
        kf = kb.astype(F32)
        qf = qb.astype(F32)
        vf = v_ref[0, hh].astype(F32).reshape(n, c, d)
        kk = _bdot('ncd,nsd->ncs', kb, kb)
        qk = _bdot('ncd,nsd->ncs', qb, kb)
        for di in range(2):
            col = lambda idx: jnp.sum(jnp.where(lane == idx * HEADS + h, gt, 0.0), axis=-1,
                                      keepdims=True).reshape(n, c, 1)
            beta = col(2 * di)
            gcol = col(2 * di + 1)
            grow = grow_ref[0, 2 * di + 1, hh]
            incl = (ii >= jj) if di == 0 else (ii <= jj)
            strict = (ii > jj) if di == 0 else (ii < jj)
            decay = jnp.where(incl, jnp.exp(jnp.where(incl, gcol - grow, 0.0)), 0.0)
            a = jnp.where(strict, beta * kk * decay, 0.0)
            tinv = _unit_tri_inverse(a, c)
            egc = jnp.exp(gcol)
            rhs = jnp.concatenate([vf * beta, kf * (beta * egc)], axis=-1).astype(BF16)
            uw = _bdot('ncs,nsd->ncd', tinv.astype(BF16), rhs).astype(BF16)
            iw = _bdot('ncs,nsd->ncd', (qk * decay).astype(BF16), uw)
            qp = qf * egc - iw[..., d:]
            glast = gcol[:, c - 1:c, :] if di == 0 else gcol[:, 0:1, :]
            kt = (kf * jnp.exp(glast - gcol)).astype(BF16)
            ktuw = _bdot('ncd,nce->nde', kt, uw)
            m_scr[di][hh] = (jnp.exp(glast) * eye_d - ktuw[..., d:]).astype(BF16)
            n_scr[di][hh] = ktuw[..., :d]
            qp_scr[di][hh] = qp.reshape(t, d).astype(BF16)
            if di == 0:
                oacc_scr[hh] = iw[..., :d].reshape(t, d)
            else:
                oacc_scr[hh] += iw[..., :d].reshape(t, d)
        return carry

    lax.fori_loop(0, hb, precompute, 0)

    for hh in range(hb):
        sf_scr[hh, 0] = jnp.zeros((d, d), BF16)
        sb_scr[hh, n - 1] = jnp.zeros((d, d), BF16)

    def scan(i, carry):
        nb = n - 1 - i
        ins = [(mf_scr[hh, i], sf_scr[hh, i], nf_scr[hh, i], mb_scr[hh, nb], sb_scr[hh, nb], nb_scr[hh, nb])
               for hh in range(hb)]
        outs = [((_dot(mf, sf) + nf).astype(BF16), (_dot(mb, sb) + nbv).astype(BF16))
                for mf, sf, nf, mb, sb, nbv in ins]
        for hh, (of, ob) in enumerate(outs):
            sf_scr[hh, i + 1] = of
            sb_scr[hh, nb - 1] = ob
        return carry

    lax.fori_loop(0, n - 1, scan, 0)

    def finish(hh, carry):
        qp = jnp.concatenate([qpf_scr[hh], qpb_scr[hh]], axis=-1).reshape(n, c, 2 * d)
        st = jnp.concatenate([sf_scr[hh], sb_scr[hh]], axis=-2)
        o = oacc_scr[hh] + _bdot('nck,nkd->ncd', qp, st).reshape(t, d)
        o_ref[0, hh] = (_rms(o, nw_ref[...]) * _silu(z_ref[0, hh])).astype(BF16)
        return carry

    lax.fori_loop(0, hb, finish, 0)


def _gdn_core(q, k, v, z, g, grow, norm_w):
    bsz, _, t, d = q.shape
    c, hb = GDN_CHUNK, GDN_HEADS_PER_STEP
    n = t // c
    grow = grow.reshape(bsz, 4, HEADS, n, 1, c)
    head = pl.BlockSpec((1, hb, t, d), lambda b, h: (b, h, 0, 0))
    mats = lambda dt: pltpu.VMEM((hb, n, d, d), dt)
    return pl.pallas_call(
        functools.partial(_gdn_core_kernel, t=t, c=c, hb=hb),
        grid=(bsz, HEADS // hb),
        in_specs=[head, head, head, head,
                  pl.BlockSpec((1, t, LANES), lambda b, h: (b, 0, 0)),
                  pl.BlockSpec((1, 4, hb, n, 1, c), lambda b, h: (b, 0, h, 0, 0, 0)),
                  pl.BlockSpec((1, d), lambda b, h: (0, 0))],
        out_specs=head,
        out_shape=jax.ShapeDtypeStruct((bsz, HEADS, t, d), BF16),
        scratch_shapes=[mats(BF16), mats(BF16), mats(F32), mats(F32),
                        pltpu.VMEM((hb, t, d), BF16), pltpu.VMEM((hb, t, d), BF16),
                        mats(BF16), mats(BF16), pltpu.VMEM((hb, t, d), F32)],
        compiler_params=_params("parallel", "parallel"),
        name="gdn_core",
    )(q, k, v, z, g, grow, norm_w[None])


def _router_epilogue(x1, nffn_ref, wr_ref, hn_ref, lgt_ref):
    hn = _rms(x1, nffn_ref[...])
    hi = hn.astype(BF16)
    lo = (hn - hi.astype(F32)).astype(BF16)
    _to_token_major(hn_ref.at[0], hn)
    lg = _dot(hi, wr_ref[0]) + _dot(hi, wr_ref[1]) + _dot(lo, wr_ref[0])
    lgt_ref[0] = lg.T[:N_EXPERTS]


def _router_weights(w_router):
    wp = jnp.pad(w_router, ((0, 0), (0, LANES - N_EXPERTS)))
    hi = wp.astype(BF16)
    lo = (wp - hi.astype(F32)).astype(BF16)
    return jnp.stack([hi, lo])


def _outproj_kernel(og_ref, x_ref, wout_ref, nffn_ref, wr_ref, x1_ref, hn_ref, lgt_ref):
    og = jnp.concatenate([og_ref[0, h] for h in range(HEADS)], axis=-1)
    x1 = x_ref[0] + _dot(og, wout_ref[...])
    x1_ref[0] = x1
    _router_epilogue(x1, nffn_ref, wr_ref, hn_ref, lgt_ref)


def _router_out(bsz, t, d, tm):
    rows = d // LANES
    specs = [pl.BlockSpec((1, tm, d), lambda b, tt: (b, tt, 0)),
             pl.BlockSpec((1, tm * rows, LANES), lambda b, tt: (b, tt, 0)),
             pl.BlockSpec((1, N_EXPERTS, tm), lambda b, tt: (b, 0, tt))]
    shapes = [jax.ShapeDtypeStruct((bsz, t, d), F32), jax.ShapeDtypeStruct((bsz, t * rows, LANES), F32),
              jax.ShapeDtypeStruct((bsz, N_EXPERTS, t), F32)]
    return specs, shapes


def _outproj_router(og, x, w_out, norm_ffn, w_router, *, tm=512):
    bsz, t, d = x.shape
    full = lambda shape: pl.BlockSpec(shape, lambda b, tt: (0,) * len(shape))
    out_specs, out_shape = _router_out(bsz, t, d, tm)
    return pl.pallas_call(
        _outproj_kernel,
        grid=(bsz, t // tm),
        in_specs=[pl.BlockSpec((1, HEADS, tm, HEAD_DIM), lambda b, tt: (b, 0, tt, 0)),
                  pl.BlockSpec((1, tm, d), lambda b, tt: (b, tt, 0)),
                  full(w_out.shape), full((1, d)), full((2, d, LANES))],
        out_specs=out_specs,
        out_shape=out_shape,
        compiler_params=_params("parallel", "parallel"),
        name="outproj_router",
    )(og, x, w_out.astype(BF16), norm_ffn[None], _router_weights(w_router))


def _conformer_kernel(xp_ref, xc_ref, xn_ref, nmix_ref, w1_ref, b1_ref, dww_ref, dwb_ref, lng_ref, lnb_ref,
                      w2_ref, b2_ref, nffn_ref, wr_ref, x1_ref, hn_ref, lgt_ref, u_scr, c_scr,
                      *, tm, nt, halo):
    d = xc_ref.shape[-1]
    x_ext = jnp.concatenate([xp_ref[0], xc_ref[0], xn_ref[0]], axis=0)
    hn = _rms(x_ext, nmix_ref[...]).astype(BF16)
    pg = _dot(hn, w1_ref[...]) + b1_ref[...]
    u = pg[:, :d] * _sigmoid(pg[:, d:])
    _to_token_major(u_scr, jnp.where(_ext_valid(tm, halo, nt), u, 0.0))
    _depthwise_conv_blocks(u_scr, c_scr, lambda j: dww_ref[j], CF_CONV, halo - CF_CONV // 2, tm, lambda a: a)
    conv = jnp.concatenate(_from_token_major(c_scr, tm), axis=-1) + dwb_ref[...]
    xc = conv - jnp.mean(conv, axis=-1, keepdims=True)
    y = xc * lax.rsqrt(jnp.mean(xc * xc, axis=-1, keepdims=True) + EPS) * lng_ref[...] + lnb_ref[...]
    mix = _dot(_silu(y).astype(BF16), w2_ref[...]) + b2_ref[...]
    x1 = xc_ref[0] + mix
    x1_ref[0] = x1
    _router_epilogue(x1, nffn_ref, wr_ref, hn_ref, lgt_ref)


def _conformer_router(x, norm_mix, w_pw1, b_pw1, dw_w, dw_b, ln_g, ln_b, w_pw2, b_pw2, norm_ffn, w_router,
                      *, tm=256, halo=16):
    bsz, t, d = x.shape
    nt = t // tm
    prev, cur, nxt = _halo_specs(tm, halo, t // halo, d)
    full = lambda shape: pl.BlockSpec(shape, lambda b, tt: (0,) * len(shape))
    dww = jnp.pad(dw_w, ((0, 32 - CF_CONV), (0, 0))).reshape(32, SUBLANES, LANES)
    out_specs, out_shape = _router_out(bsz, t, d, tm)
    return pl.pallas_call(
        functools.partial(_conformer_kernel, tm=tm, nt=nt, halo=halo),
        grid=(bsz, nt),
        in_specs=[prev, cur, nxt, full((1, d)), full((d, 2 * d)), full((1, 2 * d)),
                  full((32, SUBLANES, LANES)), full((1, d)), full((1, d)), full((1, d)), full((d, d)),
                  full((1, d)), full((1, d)), full((2, d, LANES))],
        out_specs=out_specs,
        out_shape=out_shape,
        scratch_shapes=[pltpu.VMEM(((tm + 2 * halo) * SUBLANES, LANES), F32),
                        pltpu.VMEM((tm * SUBLANES, LANES), F32)],
        compiler_params=_params("parallel", "parallel"),
        name="conformer_router",
    )(x, x, x, norm_mix[None], w_pw1.astype(BF16), b_pw1[None], dww, dw_b[None], ln_g[None], ln_b[None],
      w_pw2.astype(BF16), b_pw2[None], norm_ffn[None], _router_weights(w_router))


def _route_kernel(lg_ref, tok_ref, post_ref, afft_ref, tri_scr, *, t, cap):
    @pl.when(pl.program_id(0) == 0)
    def _():
        r = lax.broadcasted_iota(jnp.int32, (t, t), 0)
        cidx = lax.broadcasted_iota(jnp.int32, (t, t), 1)
        tri_scr[...] = jnp.where(r < cidx, 1.0, 0.0).astype(BF16)

    lt = lg_ref[0]
    ne = lt.shape[0]
    e = jnp.exp(lt - jnp.max(lt, axis=0, keepdims=True))
    p = e / jnp.sum(e, axis=0, keepdims=True)
    bits = pltpu.bitcast(p, jnp.int32)
    count = lambda m: jnp.sum(jnp.where(m, 1.0, 0.0), axis=1, keepdims=True)
    thr = jnp.zeros((ne, 1), jnp.int32)
    for bit in range(30, -1, -1):
        cand = thr | (1 << bit)
        thr = jnp.where(count(bits >= cand) >= cap, cand, thr)
    gt = bits > thr
    eq = bits == thr
    need = cap - count(gt)
    eq_before = _dot(jnp.where(eq, 1.0, 0.0).astype(BF16), tri_scr[...])
    sel = gt | (eq & (eq_before < need))
    slot = _dot(jnp.where(sel, 1.0, 0.0).astype(BF16), tri_scr[...])
    pos = jnp.where(sel, slot.astype(jnp.int32), -1)
    post_ref[0] = jnp.concatenate([pos, jnp.full((LANES - ne, t), -1, jnp.int32)], axis=0).T
    trow = lax.broadcasted_iota(jnp.int32, (1, t), 1).astype(F32)
    slot_sub = lax.broadcasted_iota(jnp.int32, (cap, 1), 0)
    lane = lax.broadcasted_iota(jnp.int32, (1, LANES), 1)
    tokm = jnp.zeros((cap, LANES), F32)
    affm = jnp.zeros((cap, LANES), F32)
    for ei in range(ne):
        match = slot_sub == pos[ei:ei + 1, :]
        tokm = tokm + jnp.where(lane == ei, jnp.sum(jnp.where(match, trow, 0.0), axis=1, keepdims=True), 0.0)
        affm = affm + jnp.where(lane == ei, jnp.sum(jnp.where(match, p[ei:ei + 1, :], 0.0), axis=1,
                                                    keepdims=True), 0.0)
    tok_ref[0] = tokm.T[:ne].astype(jnp.int32)
    afft_ref[0] = affm


def _route(logits_t, cap):
    bsz, e, t = logits_t.shape
    return pl.pallas_call(
        functools.partial(_route_kernel, t=t, cap=cap),
        grid=(bsz,),
        in_specs=[pl.BlockSpec((1, e, t), lambda b: (b, 0, 0))],
        out_specs=[pl.BlockSpec((1, e, cap), lambda b: (b, 0, 0)),
                   pl.BlockSpec((1, t, LANES), lambda b: (b, 0, 0)),
                   pl.BlockSpec((1, cap, LANES), lambda b: (b, 0, 0))],
        out_shape=[jax.ShapeDtypeStruct((bsz, e, cap), jnp.int32),
                   jax.ShapeDtypeStruct((bsz, t, LANES), jnp.int32),
                   jax.ShapeDtypeStruct((bsz, cap, LANES), F32)],
        scratch_shapes=[pltpu.VMEM((t, t), BF16)],
        compiler_params=_params("arbitrary"),
        name="route",
    )(logits_t)


def _gather_kernel(tok_ref, hn_ref, x_ref, rows_scr, *, cap, ne):
    def per_expert(e, carry):
        for s in range(cap):
            start = pl.multiple_of(tok_ref[0, 0, e * cap + s] * SUBLANES, SUBLANES)
            rows_scr[pl.ds(s * SUBLANES, SUBLANES), :] = hn_ref[0, pl.ds(start, SUBLANES), :]
        x_ref[e, 0] = jnp.concatenate(_from_token_major(rows_scr, cap), axis=-1).astype(BF16)
        return carry

    lax.fori_loop(0, ne, per_expert, 0)


def _gather(hn_tm, tok, cap):
    bsz, rows, _ = hn_tm.shape
    e = tok.shape[1]
    d = SUBLANES * LANES
    return pl.pallas_call(
        functools.partial(_gather_kernel, cap=cap, ne=e),
        grid=(bsz,),
        in_specs=[pl.BlockSpec((1, 1, e * cap), lambda b: (b, 0, 0), memory_space=pltpu.SMEM),
                  pl.BlockSpec((1, rows, LANES), lambda b: (b, 0, 0))],
        out_specs=pl.BlockSpec((e, 1, cap, d), lambda b: (0, b, 0, 0)),
        out_shape=jax.ShapeDtypeStruct((e, bsz, cap, d), BF16),
        scratch_shapes=[pltpu.VMEM((cap * SUBLANES, LANES), F32)],
        compiler_params=_params("parallel"),
        name="gather",
    )(tok.reshape(bsz, 1, e * cap), hn_tm)


def _expert_kernel(x_ref, afft_ref, wg_ref, wu_ref, wd_ref, y_ref, wgb, wub, wdb):
    @pl.when(pl.program_id(1) == 0)
    def _():
        wgb[...] = wg_ref[0, 0].astype(BF16)
        wub[...] = wu_ref[0, 0].astype(BF16)
        wdb[...] = wd_ref[0, 0].astype(BF16)

    _, nb, cap, d = x_ref.shape
    xin = x_ref[0].reshape(nb * cap, d)
    lane = lax.broadcasted_iota(jnp.int32, (1, 1, LANES), 2)
    aff = jnp.sum(jnp.where(lane == pl.program_id(0), afft_ref[...], 0.0), axis=-1, keepdims=True)
    hid = (_silu(_dot(xin, wgb[...])) * _dot(xin, wub[...])).astype(BF16)
    y = _dot(hid, wdb[...]) * aff.reshape(nb * cap, 1)
    y_ref[0] = y.astype(BF16).reshape(nb, cap, d)


def _experts(xin, aff_t, w_gate, w_up, w_down, layer, *, nb=2):
    e, bsz, cap, d = xin.shape
    ff = w_gate.shape[-1]
    rows = pl.BlockSpec((1, nb, cap, d), lambda ei, b: (ei, b, 0, 0))
    return pl.pallas_call(
        _expert_kernel,
        grid=(e, bsz // nb),
        in_specs=[rows, pl.BlockSpec((nb, cap, LANES), lambda ei, b: (b, 0, 0)),
                  pl.BlockSpec((1, 1, d, ff), lambda ei, b: (layer, ei, 0, 0)),
                  pl.BlockSpec((1, 1, d, ff), lambda ei, b: (layer, ei, 0, 0)),
                  pl.BlockSpec((1, 1, ff, d), lambda ei, b: (layer, ei, 0, 0))],
        out_specs=rows,
        out_shape=jax.ShapeDtypeStruct((e, bsz, cap, d), BF16),
        scratch_shapes=[pltpu.VMEM((d, ff), BF16), pltpu.VMEM((d, ff), BF16), pltpu.VMEM((ff, d), BF16)],
        compiler_params=_params("arbitrary", "arbitrary"),
        name="experts",
    )(xin, aff_t, w_gate, w_up, w_down)


def _combine_kernel(x_ref, post_ref, y_ref, fw_ref, o_ref, *, cap, final):
    e = y_ref.shape[0]
    d = y_ref.shape[-1]
    tm = x_ref.shape[1]
    post = post_ref[0]
    lane = lax.broadcasted_iota(jnp.int32, (tm, cap), 1)
    onehot = jnp.concatenate(
        [jnp.where(lane == post[:, ei:ei + 1], 1.0, 0.0).astype(BF16) for ei in range(e)], axis=1)
    out = x_ref[0] + _dot(onehot, y_ref[:, 0].reshape(e * cap, d))
    if final:
        out = _rms(out, fw_ref[...])
    o_ref[0] = out


def _combine(x, pos_t, y, final_w, cap, *, final, tm=512):
    bsz, t, d = x.shape
    e = y.shape[0]
    return pl.pallas_call(
        functools.partial(_combine_kernel, cap=cap, final=final),
        grid=(bsz, t // tm),
        in_specs=[pl.BlockSpec((1, tm, d), lambda b, tt: (b, tt, 0)),
                  pl.BlockSpec((1, tm, LANES), lambda b, tt: (b, tt, 0)),
                  pl.BlockSpec((e, 1, cap, d), lambda b, tt: (0, b, 0, 0)),
                  pl.BlockSpec((1, d), lambda b, tt: (0, 0))],
        out_specs=pl.BlockSpec((1, tm, d), lambda b, tt: (b, tt, 0)),
        out_shape=jax.ShapeDtypeStruct((bsz, t, d), F32),
        compiler_params=_params("parallel", "parallel"),
        name="combine",
    )(x, pos_t, y, final_w[None])


def _moe(x1, hn, logits_t, w_gate, w_up, w_down, layer, final_w, *, final):
    t = x1.shape[1]
    cap = CAPACITY_FACTOR * t // N_EXPERTS
    tok, pos_t, aff_t = _route(logits_t, cap)
    y = _experts(_gather(hn, tok, cap), aff_t, w_gate, w_up, w_down, layer)
    return _combine(x1, pos_t, y, final_w, cap, final=final)


def kernel(x, norm_mix, norm_ffn, dn_w_in, dn_conv_w, dn_a_log, dn_dt_bias, dn_norm_w, dn_w_out, cf_w_pw1, cf_b_pw1, cf_dw_w, cf_dw_b, cf_ln_g, cf_ln_b, cf_w_pw2, cf_b_pw2, moe_w_router, moe_w_gate, moe_w_up, moe_w_down, final_norm):
    q, k, v, z, g, grow = _gdn_pre(x, norm_mix[0], dn_w_in[0], dn_conv_w[0], dn_a_log[0], dn_dt_bias[0])
    og = _gdn_core(q, k, v, z, g, grow, dn_norm_w[0])
    x1, hn, lgt = _outproj_router(og, x, dn_w_out[0], norm_ffn[0], moe_w_router[0])
    x2 = _moe(x1, hn, lgt, moe_w_gate, moe_w_up, moe_w_down, 0, final_norm, final=False)
    x3, hn, lgt = _conformer_router(x2, norm_mix[1], cf_w_pw1[0], cf_b_pw1[0], cf_dw_w[0], cf_dw_b[0],
                                    cf_ln_g[0], cf_ln_b[0], cf_w_pw2[0], cf_b_pw2[0], norm_ffn[1],
                                    moe_w_router[1])
    return _moe(x3, hn, lgt, moe_w_gate, moe_w_up, moe_w_down, 1, final_norm, final=True)
```

```python
import functools

import jax
import jax.numpy as jnp
from jax import lax
from jax.experimental import pallas as pl
from jax.experimental.pallas import tpu as pltpu

F32 = jnp.float32
BF16 = jnp.bfloat16
EPS = 1e-6

LANES = 128
SUBLANES = 8
HEADS = 8
HEAD_DIM = 128
DN_CONV = 5
CF_CONV = 31
N_EXPERTS = 16
CAPACITY_FACTOR = 2
GDN_CHUNK = 128
GDN_HEADS_PER_STEP = 2
CONV_BLOCK = 16
VMEM_LIMIT = 56 * 1024 * 1024


def _params(*sem):
    return pltpu.CompilerParams(dimension_semantics=sem, vmem_limit_bytes=VMEM_LIMIT)


def _sigmoid(x):
    return 1.0 / (1.0 + jnp.exp(-x))


def _silu(x):
    return x * _sigmoid(x)


def _softplus(x):
    return jnp.maximum(x, 0.0) + jnp.log1p(jnp.exp(-jnp.abs(x)))


def _rms(x, w):
    return x * lax.rsqrt(jnp.mean(x * x, axis=-1, keepdims=True) + EPS) * w


def _dot(a, b):
    return jnp.dot(a, b, preferred_element_type=F32)


def _bdot(spec, a, b):
    return jnp.einsum(spec, a, b, preferred_element_type=F32)


def _seg_cumsum(x, seg, reverse):
    n = x.shape[0]
    r = lax.broadcasted_iota(jnp.int32, (n, 1), 0) % seg
    s = 1
    while s < seg:
        if reverse:
            x = x + jnp.where(r < seg - s, pltpu.roll(x, n - s, axis=0), 0.0)
        else:
            x = x + jnp.where(r >= s, pltpu.roll(x, s, axis=0), 0.0)
        s *= 2
    return x


def _halo_specs(tm, halo, nt_total, d):
    per = tm // halo
    prev = pl.BlockSpec((1, halo, d), lambda b, t: (b, jnp.maximum(t * per - 1, 0), 0))
    cur = pl.BlockSpec((1, tm, d), lambda b, t: (b, t, 0))
    nxt = pl.BlockSpec((1, halo, d), lambda b, t: (b, jnp.minimum((t + 1) * per, nt_total - 1), 0))
    return prev, cur, nxt


def _ext_valid(tm, halo, nt):
    t = pl.program_id(1)
    rows = lax.broadcasted_iota(jnp.int32, (tm + 2 * halo, 1), 0)
    lo = jnp.where(t == 0, halo, 0)
    hi = jnp.where(t == nt - 1, tm + halo, tm + 2 * halo)
    return (rows >= lo) & (rows < hi)


def _to_token_major(dst_ref, x):
    rows = x.shape[0]
    for j in range(SUBLANES):
        dst_ref[pl.ds(j, rows, stride=SUBLANES), :] = x[:, j * LANES:(j + 1) * LANES]


def _from_token_major(src_ref, rows):
    return [src_ref[pl.ds(j, rows, stride=SUBLANES), :] for j in range(SUBLANES)]


def _depthwise_conv_blocks(src_ref, dst_ref, tap, n_taps, first_row, tm, post):
    rows = CONV_BLOCK * SUBLANES
    for i in range(tm // CONV_BLOCK):
        acc = jnp.zeros((CONV_BLOCK, SUBLANES, LANES), F32)
        for j in range(n_taps):
            start = (i * CONV_BLOCK + first_row + j) * SUBLANES
            seg = src_ref[pl.ds(start, rows), :].reshape(CONV_BLOCK, SUBLANES, LANES)
            acc = acc + seg * tap(j)[None]
        dst_ref[pl.ds(i * rows, rows), :] = post(acc.reshape(rows, LANES))


def _gdn_pre_kernel(xp_ref, xc_ref, xn_ref, nw_ref, wqkvz_ref, wg_ref, cw_ref, alog_ref, dtb_ref,
                    q_ref, k_ref, v_ref, z_ref, g_ref, grow_ref, ext_ref, act_ref,
                    *, tm, nt, halo, chunk):
    width = HEADS * HEAD_DIM
    x_ext = jnp.concatenate([xp_ref[0], xc_ref[0], xn_ref[0]], axis=0)
    hn = _rms(x_ext, nw_ref[...]).astype(BF16)
    proj = _dot(hn, wqkvz_ref[...])
    valid = _ext_valid(tm, halo, nt)
    for h in range(HEADS):
        z_ref[0, h] = proj[halo:halo + tm, 3 * width + h * HEAD_DIM: 3 * width + (h + 1) * HEAD_DIM]

    l2 = lambda a: a * lax.rsqrt(jnp.sum(a * a, axis=-1, keepdims=True) + EPS)
    posts = (lambda a: l2(a) * (HEAD_DIM ** -0.5), l2, lambda a: a)
    for grp, out_ref in enumerate((q_ref, k_ref, v_ref)):
        _to_token_major(ext_ref, jnp.where(valid, proj[:, grp * width:(grp + 1) * width], 0.0))
        _depthwise_conv_blocks(ext_ref, act_ref, lambda j: cw_ref[j, grp], DN_CONV,
                               halo - DN_CONV // 2, tm, _silu)
        for h, tile in enumerate(_from_token_major(act_ref, tm)):
            out_ref[0, h] = posts[grp](tile).astype(BF16)

    graw = _dot(hn[halo:halo + tm], wg_ref[...])
    lane = lax.broadcasted_iota(jnp.int32, (1, LANES), 1)
    beta = _sigmoid(graw)
    logg = -jnp.exp(alog_ref[...]) * _softplus(graw + dtb_ref[...])
    cum_f = _seg_cumsum(logg, chunk, reverse=False)
    cum_b = _seg_cumsum(logg, chunk, reverse=True)
    is_beta = ((lane % (2 * HEADS)) < HEADS) & (lane < 4 * HEADS)
    is_f = (lane >= HEADS) & (lane < 2 * HEADS)
    is_b = (lane >= 3 * HEADS) & (lane < 4 * HEADS)
    g = jnp.where(is_beta, beta, jnp.where(is_f, cum_f, jnp.where(is_b, cum_b, 0.0)))
    g_ref[0] = g
    grow_ref[0] = g.T[:4 * HEADS]


def _gdn_pre(x, norm_w, w_in, conv_w, a_log, dt_bias, *, tm=256, halo=8):
    bsz, t, d = x.shape
    width = HEADS * HEAD_DIM
    nt = t // tm
    wqkvz = w_in[:, :4 * width].astype(BF16)
    wg = jnp.pad(w_in[:, 4 * width:], ((0, 0), (0, LANES - 4 * HEADS))).astype(BF16)
    cw = jnp.pad(conv_w, ((0, 8 - DN_CONV), (0, 0))).reshape(8, 3, HEADS, HEAD_DIM)
    zeros = jnp.zeros((HEADS,), F32)
    alog = jnp.pad(jnp.concatenate([zeros, a_log[0], zeros, a_log[1]]), (0, LANES - 4 * HEADS))[None]
    dtb = jnp.pad(jnp.concatenate([zeros, dt_bias[0], zeros, dt_bias[1]]), (0, LANES - 4 * HEADS))[None]
    prev, cur, nxt = _halo_specs(tm, halo, t // halo, d)
    full = lambda shape: pl.BlockSpec(shape, lambda b, tt: (0,) * len(shape))
    heads = pl.BlockSpec((1, HEADS, tm, HEAD_DIM), lambda b, tt: (b, 0, tt, 0))
    hshape = (bsz, HEADS, t, HEAD_DIM)
    return pl.pallas_call(
        functools.partial(_gdn_pre_kernel, tm=tm, nt=nt, halo=halo, chunk=GDN_CHUNK),
        grid=(bsz, nt),
        in_specs=[prev, cur, nxt, full((1, d)), full((d, 4 * width)), full((d, LANES)),
                  full((8, 3, HEADS, HEAD_DIM)), full((1, LANES)), full((1, LANES))],
        out_specs=[heads, heads, heads, heads,
                   pl.BlockSpec((1, tm, LANES), lambda b, tt: (b, tt, 0)),
                   pl.BlockSpec((1, 4 * HEADS, tm), lambda b, tt: (b, 0, tt))],
        out_shape=[jax.ShapeDtypeStruct(hshape, BF16)] * 3
        + [jax.ShapeDtypeStruct(hshape, F32), jax.ShapeDtypeStruct((bsz, t, LANES), F32),
           jax.ShapeDtypeStruct((bsz, 4 * HEADS, t), F32)],
        scratch_shapes=[pltpu.VMEM(((tm + 2 * halo) * SUBLANES, LANES), F32),
                        pltpu.VMEM((tm * SUBLANES, LANES), F32)],
        compiler_params=_params("parallel", "parallel"),
        name="gdn_pre",
    )(x, x, x, norm_w[None], wqkvz, wg, cw, alog, dtb)


def _unit_tri_inverse(a, c):
    ii = lax.broadcasted_iota(jnp.int32, (c, c), 0)
    jj = lax.broadcasted_iota(jnp.int32, (c, c), 1)
    eye = (ii == jj).astype(F32)
    x = eye - jnp.where((ii // 2) == (jj // 2), a, 0.0)
    s = 2
    while s < c:
        off = ((ii // (2 * s)) == (jj // (2 * s))) & ((ii // s) != (jj // s))
        a_l = jnp.where(off, a, 0.0).astype(BF16)
        y = _bdot('nij,njk->nik', x.astype(BF16), a_l)
        x = x - _bdot('nij,njk->nik', y.astype(BF16), x.astype(BF16))
        s *= 2
    return x


def _gdn_core_kernel(q_ref, k_ref, v_ref, z_ref, g_ref, grow_ref, nw_ref, o_ref,
                     mf_scr, mb_scr, nf_scr, nb_scr, qpf_scr, qpb_scr, sf_scr, sb_scr, oacc_scr,
                     *, t, c, hb):
    n = t // c
    d = HEAD_DIM
    lane = lax.broadcasted_iota(jnp.int32, (1, LANES), 1)
    ii = lax.broadcasted_iota(jnp.int32, (c, c), 0)
    jj = lax.broadcasted_iota(jnp.int32, (c, c), 1)
    eye_d = (lax.broadcasted_iota(jnp.int32, (d, d), 0) == lax.broadcasted_iota(jnp.int32, (d, d), 1)).astype(F32)
    m_scr, n_scr, qp_scr = (mf_scr, mb_scr), (nf_scr, nb_scr), (qpf_scr, qpb_scr)

    def precompute(hh, carry):
        h = pl.program_id(1) * hb + hh
        gt = g_ref[0]
        kb = k_ref[0, hh].reshape(n, c, d)
        qb = q_ref[0, hh].reshape(n, c, d)
        kf = kb.astype(F32)
        qf = qb.astype(F32)
        vf = v_ref[0, hh].astype(F32).reshape(n, c, d)
        kk = _bdot('ncd,nsd->ncs', kb, kb)
        qk = _bdot('ncd,nsd->ncs', qb, kb)
        for di in range(2):
            col = lambda idx: jnp.sum(jnp.where(lane == idx * HEADS + h, gt, 0.0), axis=-1,
                                      keepdims=True).reshape(n, c, 1)
            beta = col(2 * di)
            gcol = col(2 * di + 1)
            grow = grow_ref[0, 2 * di + 1, hh]
            incl = (ii >= jj) if di == 0 else (ii <= jj)
            strict = (ii > jj) if di == 0 else (ii < jj)
            decay = jnp.where(incl, jnp.exp(jnp.where(incl, gcol - grow, 0.0)), 0.0)
            a = jnp.where(strict, beta * kk * decay, 0.0)
            tinv = _unit_tri_inverse(a, c)
            egc = jnp.exp(gcol)
            rhs = jnp.concatenate([vf * beta, kf * (beta * egc)], axis=-1).astype(BF16)
            uw = _bdot('ncs,nsd->ncd', tinv.astype(BF16), rhs).astype(BF16)
            iw = _bdot('ncs,nsd->ncd', (qk * decay).astype(BF16), uw)
            qp = qf * egc - iw[..., d:]
            glast = gcol[:, c - 1:c, :] if di == 0 else gcol[:, 0:1, :]
            kt = (kf * jnp.exp(glast - gcol)).astype(BF16)
            ktuw = _bdot('ncd,nce->nde', kt, uw)
            m_scr[di][hh] = (jnp.exp(glast) * eye_d - ktuw[..., d:]).astype(BF16)
            n_scr[di][hh] = ktuw[..., :d]
            qp_scr[di][hh] = qp.reshape(t, d).astype(BF16)
            if di == 0:
                oacc_scr[hh] = iw[..., :d].reshape(t, d)
            else:
                oacc_scr[hh] += iw[..., :d].reshape(t, d)
        return carry

    lax.fori_loop(0, hb, precompute, 0)

    for hh in range(hb):
        sf_scr[hh, 0] = jnp.zeros((d, d), BF16)
        sb_scr[hh, n - 1] = jnp.zeros((d, d), BF16)

    def scan(i, carry):
        nb = n - 1 - i
        ins = [(mf_scr[hh, i], sf_scr[hh, i], nf_scr[hh, i], mb_scr[hh, nb], sb_scr[hh, nb], nb_scr[hh, nb])
               for hh in range(hb)]
        outs = [((_dot(mf, sf) + nf).astype(BF16), (_dot(mb, sb) + nbv).astype(BF16))
                for mf, sf, nf, mb, sb, nbv in ins]
        for hh, (of, ob) in enumerate(outs):
            sf_scr[hh, i + 1] = of
            sb_scr[hh, nb - 1] = ob
        return carry

    lax.fori_loop(0, n - 1, scan, 0)

    def finish(hh, carry):
        qp = jnp.concatenate([qpf_scr[hh], qpb_scr[hh]], axis=-1).reshape(n, c, 2 * d)
        st = jnp.concatenate([sf_scr[hh], sb_scr[hh]], axis=-2)
        o = oacc_scr[hh] + _bdot('nck,nkd->ncd', qp, st).reshape(t, d)
        o_ref[0, hh] = (_rms(o, nw_ref[...]) * _silu(z_ref[0, hh])).astype(BF16)
        return carry

    lax.fori_loop(0, hb, finish, 0)


def _gdn_core(q, k, v, z, g, grow, norm_w):
    bsz, _, t, d = q.shape
    c, hb = GDN_CHUNK, GDN_HEADS_PER_STEP
    n = t // c
    grow = grow.reshape(bsz, 4, HEADS, n, 1, c)
    head = pl.BlockSpec((1, hb, t, d), lambda b, h: (b, h, 0, 0))
    mats = lambda dt: pltpu.VMEM((hb, n, d, d), dt)
    return pl.pallas_call(
        functools.partial(_gdn_core_kernel, t=t, c=c, hb=hb),
        grid=(bsz, HEADS // hb),
        in_specs=[head, head, head, head,
                  pl.BlockSpec((1, t, LANES), lambda b, h: (b, 0, 0)),
                  pl.BlockSpec((1, 4, hb, n, 1, c), lambda b, h: (b, 0, h, 0, 0, 0)),
                  pl.BlockSpec((1, d), lambda b, h: (0, 0))],
        out_specs=head,
        out_shape=jax.ShapeDtypeStruct((bsz, HEADS, t, d), BF16),
        scratch_shapes=[mats(BF16), mats(BF16), mats(F32), mats(F32),
                        pltpu.VMEM((hb, t, d), BF16), pltpu.VMEM((hb, t, d), BF16),
                        mats(BF16), mats(BF16), pltpu.VMEM((hb, t, d), F32)],
        compiler_params=_params("parallel", "parallel"),
        name="gdn_core",
    )(q, k, v, z, g, grow, norm_w[None])


def _router_epilogue(x1, nffn_ref, wr_ref, hn_ref, lgt_ref):
    hn = _rms(x1, nffn_ref[...])
    hi = hn.astype(BF16)
    lo = (hn - hi.astype(F32)).astype(BF16)
    _to_token_major(hn_ref.at[0], hn)
    lg = _dot(hi, wr_ref[0]) + _dot(hi, wr_ref[1]) + _dot(lo, wr_ref[0])
    lgt_ref[0] = lg.T[:N_EXPERTS]


def _router_weights(w_router):
    wp = jnp.pad(w_router, ((0, 0), (0, LANES - N_EXPERTS)))
    hi = wp.astype(BF16)
    lo = (wp - hi.astype(F32)).astype(BF16)
    return jnp.stack([hi, lo])


def _outproj_kernel(og_ref, x_ref, wout_ref, nffn_ref, wr_ref, x1_ref, hn_ref, lgt_ref):
    og = jnp.concatenate([og_ref[0, h] for h in range(HEADS)], axis=-1)
    x1 = x_ref[0] + _dot(og, wout_ref[...])
    x1_ref[0] = x1
    _router_epilogue(x1, nffn_ref, wr_ref, hn_ref, lgt_ref)


def _router_out(bsz, t, d, tm):
    rows = d // LANES
    specs = [pl.BlockSpec((1, tm, d), lambda b, tt: (b, tt, 0)),
             pl.BlockSpec((1, tm * rows, LANES), lambda b, tt: (b, tt, 0)),
             pl.BlockSpec((1, N_EXPERTS, tm), lambda b, tt: (b, 0, tt))]
    shapes = [jax.ShapeDtypeStruct((bsz, t, d), F32), jax.ShapeDtypeStruct((bsz, t * rows, LANES), F32),
              jax.ShapeDtypeStruct((bsz, N_EXPERTS, t), F32)]
    return specs, shapes


def _outproj_router(og, x, w_out, norm_ffn, w_router, *, tm=512):
    bsz, t, d = x.shape
    full = lambda shape: pl.BlockSpec(shape, lambda b, tt: (0,) * len(shape))
    out_specs, out_shape = _router_out(bsz, t, d, tm)
    return pl.pallas_call(
        _outproj_kernel,
        grid=(bsz, t // tm),
        in_specs=[pl.BlockSpec((1, HEADS, tm, HEAD_DIM), lambda b, tt: (b, 0, tt, 0)),
                  pl.BlockSpec((1, tm, d), lambda b, tt: (b, tt, 0)),
                  full(w_out.shape), full((1, d)), full((2, d, LANES))],
        out_specs=out_specs,
        out_shape=out_shape,
        compiler_params=_params("parallel", "parallel"),
        name="outproj_router",
    )(og, x, w_out.astype(BF16), norm_ffn[None], _router_weights(w_router))


def _conformer_kernel(xp_ref, xc_ref, xn_ref, nmix_ref, w1_ref, b1_ref, dww_ref, dwb_ref, lng_ref, lnb_ref,
                      w2_ref, b2_ref, nffn_ref, wr_ref, x1_ref, hn_ref, lgt_ref, u_scr, c_scr,
                      *, tm, nt, halo):
    d = xc_ref.shape[-1]
    x_ext = jnp.concatenate([xp_ref[0], xc_ref[0], xn_ref[0]], axis=0)
    hn = _rms(x_ext, nmix_ref[...]).astype(BF16)
    pg = _dot(hn, w1_ref[...]) + b1_ref[...]
    u = pg[:, :d] * _sigmoid(pg[:, d:])
    _to_token_major(u_scr, jnp.where(_ext_valid(tm, halo, nt), u, 0.0))
    _depthwise_conv_blocks(u_scr, c_scr, lambda j: dww_ref[j], CF_CONV, halo - CF_CONV // 2, tm, lambda a: a)
    conv = jnp.concatenate(_from_token_major(c_scr, tm), axis=-1) + dwb_ref[...]
    xc = conv - jnp.mean(conv, axis=-1, keepdims=True)
    y = xc * lax.rsqrt(jnp.mean(xc * xc, axis=-1, keepdims=True) + EPS) * lng_ref[...] + lnb_ref[...]
    mix = _dot(_silu(y).astype(BF16), w2_ref[...]) + b2_ref[...]
    x1 = xc_ref[0] + mix
    x1_ref[0] = x1
    _router_epilogue(x1, nffn_ref, wr_ref, hn_ref, lgt_ref)


def _conformer_router(x, norm_mix, w_pw1, b_pw1, dw_w, dw_b, ln_g, ln_b, w_pw2, b_pw2, norm_ffn, w_router,
                      *, tm=256, halo=16):
    bsz, t, d = x.shape
    nt = t // tm
    prev, cur, nxt = _halo_specs(tm, halo, t // halo, d)
    full = lambda shape: pl.BlockSpec(shape, lambda b, tt: (0,) * len(shape))
    dww = jnp.pad(dw_w, ((0, 32 - CF_CONV), (0, 0))).reshape(32, SUBLANES, LANES)
    out_specs, out_shape = _router_out(bsz, t, d, tm)
    return pl.pallas_call(
        functools.partial(_conformer_kernel, tm=tm, nt=nt, halo=halo),
        grid=(bsz, nt),
        in_specs=[prev, cur, nxt, full((1, d)), full((d, 2 * d)), full((1, 2 * d)),
                  full((32, SUBLANES, LANES)), full((1, d)), full((1, d)), full((1, d)), full((d, d)),
                  full((1, d)), full((1, d)), full((2, d, LANES))],
        out_specs=out_specs,
        out_shape=out_shape,
        scratch_shapes=[pltpu.VMEM(((tm + 2 * halo) * SUBLANES, LANES), F32),
                        pltpu.VMEM((tm * SUBLANES, LANES), F32)],
        compiler_params=_params("parallel", "parallel"),
        name="conformer_router",
    )(x, x, x, norm_mix[None], w_pw1.astype(BF16), b_pw1[None], dww, dw_b[None], ln_g[None], ln_b[None],
      w_pw2.astype(BF16), b_pw2[None], norm_ffn[None], _router_weights(w_router))


def _route_kernel(lg_ref, row_ref, pidx_ref, p_ref, tri_scr, *, t, cap):
    @pl.when(pl.program_id(0) == 0)
    def _():
        r = lax.broadcasted_iota(jnp.int32, (t, t), 0)
        cidx = lax.broadcasted_iota(jnp.int32, (t, t), 1)
        tri_scr[...] = jnp.where(r < cidx, 1.0, 0.0).astype(BF16)

    lt = lg_ref[0]
    ne = lt.shape[0]
    e = jnp.exp(lt - jnp.max(lt, axis=0, keepdims=True))
    p = e / jnp.sum(e, axis=0, keepdims=True)
    p_ref[0] = p
    bits = pltpu.bitcast(p, jnp.int32)
    count = lambda m: jnp.sum(jnp.where(m, 1.0, 0.0), axis=1, keepdims=True)
    thr = jnp.zeros((ne, 1), jnp.int32)
    for bit in range(30, -1, -1):
        cand = thr | (1 << bit)
        thr = jnp.where(count(bits >= cand) >= cap, cand, thr)
    gt = bits > thr
    eq = bits == thr
    need = cap - count(gt)
    eq_before = _dot(jnp.where(eq, 1.0, 0.0).astype(BF16), tri_scr[...])
    sel = gt | (eq & (eq_before < need))
    slot = _dot(jnp.where(sel, 1.0, 0.0).astype(BF16), tri_scr[...])
    pos = jnp.where(sel, slot.astype(jnp.int32), -1)
    trow = lax.broadcasted_iota(jnp.int32, (1, t), 1).astype(F32)
    slot_sub = lax.broadcasted_iota(jnp.int32, (cap, 1), 0)
    lane = lax.broadcasted_iota(jnp.int32, (1, LANES), 1)
    tokm = jnp.zeros((cap, LANES), F32)
    for ei in range(ne):
        match = slot_sub == pos[ei:ei + 1, :]
        tokm = tokm + jnp.where(lane == ei, jnp.sum(jnp.where(match, trow, 0.0), axis=1, keepdims=True), 0.0)
    tok = tokm.T[:ne].astype(jnp.int32)
    row_ref[0] = tok * SUBLANES
    pidx_ref[0] = tok + t * lax.broadcasted_iota(jnp.int32, (ne, 1), 0)


def _route(logits_t, cap):
    bsz, e, t = logits_t.shape
    return pl.pallas_call(
        functools.partial(_route_kernel, t=t, cap=cap),
        grid=(bsz,),
        in_specs=[pl.BlockSpec((1, e, t), lambda b: (b, 0, 0))],
        out_specs=[pl.BlockSpec((1, e, cap), lambda b: (b, 0, 0)), pl.BlockSpec((1, e, cap), lambda b: (b, 0, 0)),
                   pl.BlockSpec((1, e, t), lambda b: (b, 0, 0))],
        out_shape=[jax.ShapeDtypeStruct((bsz, e, cap), jnp.int32), jax.ShapeDtypeStruct((bsz, e, cap), jnp.int32),
                   jax.ShapeDtypeStruct((bsz, e, t), F32)],
        scratch_shapes=[pltpu.VMEM((t, t), BF16)],
        compiler_params=_params("arbitrary"),
        name="route",
    )(logits_t)


def _gather_kernel(row_ref, hn_ref, x_ref, rows_scr, *, cap, ne):
    def per_expert(e, carry):
        for s in range(cap):
            start = pl.multiple_of(row_ref[0, 0, e * cap + s], SUBLANES)
            rows_scr[pl.ds(s * SUBLANES, SUBLANES), :] = hn_ref[0, pl.ds(start, SUBLANES), :]
        x_ref[e, 0] = jnp.concatenate(_from_token_major(rows_scr, cap), axis=-1).astype(BF16)
        return carry

    lax.fori_loop(0, ne, per_expert, 0)


def _gather(hn_tm, row, cap):
    bsz, rows, _ = hn_tm.shape
    e = row.shape[1]
    d = SUBLANES * LANES
    return pl.pallas_call(
        functools.partial(_gather_kernel, cap=cap, ne=e),
        grid=(bsz,),
        in_specs=[pl.BlockSpec((1, 1, e * cap), lambda b: (b, 0, 0), memory_space=pltpu.SMEM),
                  pl.BlockSpec((1, rows, LANES), lambda b: (b, 0, 0))],
        out_specs=pl.BlockSpec((e, 1, cap, d), lambda b: (0, b, 0, 0)),
        out_shape=jax.ShapeDtypeStruct((e, bsz, cap, d), BF16),
        scratch_shapes=[pltpu.VMEM((cap * SUBLANES, LANES), F32)],
        compiler_params=_params("parallel"),
        name="gather",
    )(row.reshape(bsz, 1, e * cap), hn_tm)


def _expert_kernel(x_ref, wg_ref, wu_ref, wd_ref, y_ref, wgb, wub, wdb):
    @pl.when(pl.program_id(1) == 0)
    def _():
        wgb[...] = wg_ref[0, 0].astype(BF16)
        wub[...] = wu_ref[0, 0].astype(BF16)
        wdb[...] = wd_ref[0, 0].astype(BF16)

    _, nb, cap, d = x_ref.shape
    xin = x_ref[0].reshape(nb * cap, d)
    hid = (_silu(_dot(xin, wgb[...])) * _dot(xin, wub[...])).astype(BF16)
    y = _dot(hid, wdb[...])
    for bi in range(nb):
        _to_token_major(y_ref.at[0, bi], y[bi * cap:(bi + 1) * cap])


def _experts(xin, w_gate, w_up, w_down, layer, *, nb=2):
    e, bsz, cap, d = xin.shape
    ff = w_gate.shape[-1]
    return pl.pallas_call(
        _expert_kernel,
        grid=(e, bsz // nb),
        in_specs=[pl.BlockSpec((1, nb, cap, d), lambda ei, b: (ei, b, 0, 0)),
                  pl.BlockSpec((1, 1, d, ff), lambda ei, b: (layer, ei, 0, 0)),
                  pl.BlockSpec((1, 1, d, ff), lambda ei, b: (layer, ei, 0, 0)),
                  pl.BlockSpec((1, 1, ff, d), lambda ei, b: (layer, ei, 0, 0))],
        out_specs=pl.BlockSpec((1, nb, cap * SUBLANES, LANES), lambda ei, b: (ei, b, 0, 0)),
        out_shape=jax.ShapeDtypeStruct((e, bsz, cap * SUBLANES, LANES), F32),
        scratch_shapes=[pltpu.VMEM((d, ff), BF16), pltpu.VMEM((d, ff), BF16), pltpu.VMEM((ff, d), BF16)],
        compiler_params=_params("arbitrary", "arbitrary"),
        name="experts",
    )(xin, w_gate, w_up, w_down)


SCATTER_UNROLL = 8


def _scatter_kernel(row_ref, pidx_ref, p_ref, x_ref, y_ref, fw_ref, o_ref, acc_scr, *, cap, t, eg, final):
    g = pl.program_id(1)

    @pl.when(g == 0)
    def _():
        _to_token_major(acc_scr, x_ref[0])

    def per_expert(el, carry):
        e = g * eg + el
        for s0 in range(0, cap, SCATTER_UNROLL):
            rows = []
            for s in range(s0, s0 + SCATTER_UNROLL):
                aff = p_ref[0, 0, pidx_ref[0, 0, e * cap + s]]
                start = pl.multiple_of(row_ref[0, 0, e * cap + s], SUBLANES)
                rows.append((start, acc_scr[pl.ds(start, SUBLANES), :]
                             + aff * y_ref[el, 0, pl.ds(s * SUBLANES, SUBLANES), :]))
            for start, val in rows:
                acc_scr[pl.ds(start, SUBLANES), :] = val
        return carry

    lax.fori_loop(0, eg, per_expert, 0)

    @pl.when(g == pl.num_programs(1) - 1)
    def _():
        out = jnp.concatenate(_from_token_major(acc_scr, t), axis=-1)
        o_ref[0] = _rms(out, fw_ref[...]) if final else out


def _scatter(x, row, pidx, p, y, final_w, cap, *, final, eg=2):
    bsz, t, d = x.shape
    e = y.shape[0]
    smem = lambda n: pl.BlockSpec((1, 1, n), lambda b, g: (b, 0, 0), memory_space=pltpu.SMEM)
    return pl.pallas_call(
        functools.partial(_scatter_kernel, cap=cap, t=t, eg=eg, final=final),
        grid=(bsz, e // eg),
        in_specs=[smem(e * cap), smem(e * cap), smem(e * t),
                  pl.BlockSpec((1, t, d), lambda b, g: (b, 0, 0)),
                  pl.BlockSpec((eg, 1, cap * SUBLANES, LANES), lambda b, g: (g, b, 0, 0)),
                  pl.BlockSpec((1, d), lambda b, g: (0, 0))],
        out_specs=pl.BlockSpec((1, t, d), lambda b, g: (b, 0, 0)),
        out_shape=jax.ShapeDtypeStruct((bsz, t, d), F32),
        scratch_shapes=[pltpu.VMEM((t * SUBLANES, LANES), F32)],
        compiler_params=_params("parallel", "arbitrary"),
        name="scatter",
    )(row.reshape(bsz, 1, e * cap), pidx.reshape(bsz, 1, e * cap), p.reshape(bsz, 1, e * t), x, y, final_w[None])


def _moe(x1, hn, logits_t, w_gate, w_up, w_down, layer, final_w, *, final):
    t = x1.shape[1]
    cap = CAPACITY_FACTOR * t // N_EXPERTS
    row, pidx, p = _route(logits_t, cap)
    y = _experts(_gather(hn, row, cap), w_gate, w_up, w_down, layer)
    return _scatter(x1, row, pidx, p, y, final_w, cap, final=final)


def kernel(x, norm_mix, norm_ffn, dn_w_in, dn_conv_w, dn_a_log, dn_dt_bias, dn_norm_w, dn_w_out, cf_w_pw1, cf_b_pw1, cf_dw_w, cf_dw_b, cf_ln_g, cf_ln_b, cf_w_pw2, cf_b_pw2, moe_w_router, moe_w_gate, moe_w_up, moe_w_down, final_norm):
    q, k, v, z, g, grow = _gdn_pre(x, norm_mix[0], dn_w_in[0], dn_conv_w[0], dn_a_log[0], dn_dt_bias[0])
    og = _gdn_core(q, k, v, z, g, grow, dn_norm_w[0])
    x1, hn, lgt = _outproj_router(og, x, dn_w_out[0], norm_ffn[0], moe_w_router[0])
    x2 = _moe(x1, hn, lgt, moe_w_gate, moe_w_up, moe_w_down, 0, final_norm, final=False)
    x3, hn, lgt = _conformer_router(x2, norm_mix[1], cf_w_pw1[0], cf_b_pw1[0], cf_dw_w[0], cf_dw_b[0],
                                    cf_ln_g[0], cf_ln_b[0], cf_w_pw2[0], cf_b_pw2[0], norm_ffn[1],
                                    moe_w_router[1])
    return _moe(x3, hn, lgt, moe_w_gate, moe_w_up, moe_w_down, 1, final_norm, final=True)
```

```python
import functools

import jax
import jax.numpy as jnp
from jax import lax
from jax.experimental import pallas as pl
from jax.experimental.pallas import tpu as pltpu

F32 = jnp.float32
BF16 = jnp.bfloat16
EPS = 1e-6

LANES = 128
SUBLANES = 8
HEADS = 8
HEAD_DIM = 128
DN_CONV = 5
CF_CONV = 31
N_EXPERTS = 16
CAPACITY_FACTOR = 2
GDN_CHUNK = 128
GDN_HEADS_PER_STEP = 2
CONV_BLOCK = 16
VMEM_LIMIT = 56 * 1024 * 1024


def _params(*sem):
    return pltpu.CompilerParams(dimension_semantics=sem, vmem_limit_bytes=VMEM_LIMIT)


def _sigmoid(x):
    return 1.0 / (1.0 + jnp.exp(-x))


def _silu(x):
    return x * _sigmoid(x)


def _softplus(x):
    return jnp.maximum(x, 0.0) + jnp.log1p(jnp.exp(-jnp.abs(x)))


def _rms(x, w):
    return x * lax.rsqrt(jnp.mean(x * x, axis=-1, keepdims=True) + EPS) * w


def _dot(a, b):
    return jnp.dot(a, b, preferred_element_type=F32)


def _bdot(spec, a, b):
    return jnp.einsum(spec, a, b, preferred_element_type=F32)


def _seg_cumsum(x, seg, reverse):
    n = x.shape[0]
    r = lax.broadcasted_iota(jnp.int32, (n, 1), 0) % seg
    s = 1
    while s < seg:
        if reverse:
            x = x + jnp.where(r < seg - s, pltpu.roll(x, n - s, axis=0), 0.0)
        else:
            x = x + jnp.where(r >= s, pltpu.roll(x, s, axis=0), 0.0)
        s *= 2
    return x


def _halo_specs(tm, halo, nt_total, d):
    per = tm // halo
    prev = pl.BlockSpec((1, halo, d), lambda b, t: (b, jnp.maximum(t * per - 1, 0), 0))
    cur = pl.BlockSpec((1, tm, d), lambda b, t: (b, t, 0))
    nxt = pl.BlockSpec((1, halo, d), lambda b, t: (b, jnp.minimum((t + 1) * per, nt_total - 1), 0))
    return prev, cur, nxt


def _ext_valid(tm, halo, nt):
    t = pl.program_id(1)
    rows = lax.broadcasted_iota(jnp.int32, (tm + 2 * halo, 1), 0)
    lo = jnp.where(t == 0, halo, 0)
    hi = jnp.where(t == nt - 1, tm + halo, tm + 2 * halo)
    return (rows >= lo) & (rows < hi)


def _to_token_major(dst_ref, x):
    rows = x.shape[0]
    for j in range(SUBLANES):
        dst_ref[pl.ds(j, rows, stride=SUBLANES), :] = x[:, j * LANES:(j + 1) * LANES]


def _from_token_major(src_ref, rows):
    return [src_ref[pl.ds(j, rows, stride=SUBLANES), :] for j in range(SUBLANES)]


def _depthwise_conv_blocks(src_ref, dst_ref, tap, n_taps, first_row, tm, post):
    rows = CONV_BLOCK * SUBLANES
    for i in range(tm // CONV_BLOCK):
        acc = jnp.zeros((CONV_BLOCK, SUBLANES, LANES), F32)
        for j in range(n_taps):
            start = (i * CONV_BLOCK + first_row + j) * SUBLANES
            seg = src_ref[pl.ds(start, rows), :].reshape(CONV_BLOCK, SUBLANES, LANES)
            acc = acc + seg * tap(j)[None]
        dst_ref[pl.ds(i * rows, rows), :] = post(acc.reshape(rows, LANES))


def _gdn_pre_kernel(xp_ref, xc_ref, xn_ref, nw_ref, wqkvz_ref, wg_ref, cw_ref, alog_ref, dtb_ref,
                    q_ref, k_ref, v_ref, z_ref, g_ref, grow_ref, ext_ref, act_ref,
                    *, tm, nt, halo, chunk):
    width = HEADS * HEAD_DIM
    x_ext = jnp.concatenate([xp_ref[0], xc_ref[0], xn_ref[0]], axis=0)
    hn = _rms(x_ext, nw_ref[...]).astype(BF16)
    proj = _dot(hn, wqkvz_ref[...])
    valid = _ext_valid(tm, halo, nt)
    for h in range(HEADS):
        z_ref[0, h] = proj[halo:halo + tm, 3 * width + h * HEAD_DIM: 3 * width + (h + 1) * HEAD_DIM]

    l2 = lambda a: a * lax.rsqrt(jnp.sum(a * a, axis=-1, keepdims=True) + EPS)
    posts = (lambda a: l2(a) * (HEAD_DIM ** -0.5), l2, lambda a: a)
    for grp, out_ref in enumerate((q_ref, k_ref, v_ref)):
        _to_token_major(ext_ref, jnp.where(valid, proj[:, grp * width:(grp + 1) * width], 0.0))
        _depthwise_conv_blocks(ext_ref, act_ref, lambda j: cw_ref[j, grp], DN_CONV,
                               halo - DN_CONV // 2, tm, _silu)
        for h, tile in enumerate(_from_token_major(act_ref, tm)):
            out_ref[0, h] = posts[grp](tile).astype(BF16)

    graw = _dot(hn[halo:halo + tm], wg_ref[...])
    lane = lax.broadcasted_iota(jnp.int32, (1, LANES), 1)
    beta = _sigmoid(graw)
    logg = -jnp.exp(alog_ref[...]) * _softplus(graw + dtb_ref[...])
    cum_f = _seg_cumsum(logg, chunk, reverse=False)
    cum_b = _seg_cumsum(logg, chunk, reverse=True)
    is_beta = ((lane % (2 * HEADS)) < HEADS) & (lane < 4 * HEADS)
    is_f = (lane >= HEADS) & (lane < 2 * HEADS)
    is_b = (lane >= 3 * HEADS) & (lane < 4 * HEADS)
    g = jnp.where(is_beta, beta, jnp.where(is_f, cum_f, jnp.where(is_b, cum_b, 0.0)))
    g_ref[0] = g
    grow_ref[0] = g.T[:4 * HEADS]


def _gdn_pre(x, norm_w, w_in, conv_w, a_log, dt_bias, *, tm=256, halo=8):
    bsz, t, d = x.shape
    width = HEADS * HEAD_DIM
    nt = t // tm
    wqkvz = w_in[:, :4 * width].astype(BF16)
    wg = jnp.pad(w_in[:, 4 * width:], ((0, 0), (0, LANES - 4 * HEADS))).astype(BF16)
    cw = jnp.pad(conv_w, ((0, 8 - DN_CONV), (0, 0))).reshape(8, 3, HEADS, HEAD_DIM)
    zeros = jnp.zeros((HEADS,), F32)
    alog = jnp.pad(jnp.concatenate([zeros, a_log[0], zeros, a_log[1]]), (0, LANES - 4 * HEADS))[None]
    dtb = jnp.pad(jnp.concatenate([zeros, dt_bias[0], zeros, dt_bias[1]]), (0, LANES - 4 * HEADS))[None]
    prev, cur, nxt = _halo_specs(tm, halo, t // halo, d)
    full = lambda shape: pl.BlockSpec(shape, lambda b, tt: (0,) * len(shape))
    heads = pl.BlockSpec((1, HEADS, tm, HEAD_DIM), lambda b, tt: (b, 0, tt, 0))
    hshape = (bsz, HEADS, t, HEAD_DIM)
    return pl.pallas_call(
        functools.partial(_gdn_pre_kernel, tm=tm, nt=nt, halo=halo, chunk=GDN_CHUNK),
        grid=(bsz, nt),
        in_specs=[prev, cur, nxt, full((1, d)), full((d, 4 * width)), full((d, LANES)),
                  full((8, 3, HEADS, HEAD_DIM)), full((1, LANES)), full((1, LANES))],
        out_specs=[heads, heads, heads, heads,
                   pl.BlockSpec((1, tm, LANES), lambda b, tt: (b, tt, 0)),
                   pl.BlockSpec((1, 4 * HEADS, tm), lambda b, tt: (b, 0, tt))],
        out_shape=[jax.ShapeDtypeStruct(hshape, BF16)] * 3
        + [jax.ShapeDtypeStruct(hshape, F32), jax.ShapeDtypeStruct((bsz, t, LANES), F32),
           jax.ShapeDtypeStruct((bsz, 4 * HEADS, t), F32)],
        scratch_shapes=[pltpu.VMEM(((tm + 2 * halo) * SUBLANES, LANES), F32),
                        pltpu.VMEM((tm * SUBLANES, LANES), F32)],
        compiler_params=_params("parallel", "parallel"),
        name="gdn_pre",
    )(x, x, x, norm_w[None], wqkvz, wg, cw, alog, dtb)


def _unit_tri_inverse(a, c):
    ii = lax.broadcasted_iota(jnp.int32, (c, c), 0)
    jj = lax.broadcasted_iota(jnp.int32, (c, c), 1)
    eye = (ii == jj).astype(F32)
    x = eye - jnp.where((ii // 2) == (jj // 2), a, 0.0)
    s = 2
    while s < c:
        off = ((ii // (2 * s)) == (jj // (2 * s))) & ((ii // s) != (jj // s))
        a_l = jnp.where(off, a, 0.0).astype(BF16)
        y = _bdot('nij,njk->nik', x.astype(BF16), a_l)
        x = x - _bdot('nij,njk->nik', y.astype(BF16), x.astype(BF16))
        s *= 2
    return x


def _gdn_core_kernel(q_ref, k_ref, v_ref, z_ref, g_ref, grow_ref, nw_ref, o_ref,
                     mf_scr, mb_scr, nf_scr, nb_scr, qpf_scr, qpb_scr, sf_scr, sb_scr, oacc_scr,
                     *, t, c, hb):
    n = t // c
    d = HEAD_DIM
    lane = lax.broadcasted_iota(jnp.int32, (1, LANES), 1)
    ii = lax.broadcasted_iota(jnp.int32, (c, c), 0)
    jj = lax.broadcasted_iota(jnp.int32, (c, c), 1)
    eye_d = (lax.broadcasted_iota(jnp.int32, (d, d), 0) == lax.broadcasted_iota(jnp.int32, (d, d), 1)).astype(F32)
    m_scr, n_scr, qp_scr = (mf_scr, mb_scr), (nf_scr, nb_scr), (qpf_scr, qpb_scr)

    def precompute(hh, carry):
        h = pl.program_id(1) * hb + hh
        gt = g_ref[0]
        kb = k_ref[0, hh].reshape(n, c, d)
        qb = q_ref[0, hh].reshape(n, c, d)
        kf = kb.astype(F32)
        qf = qb.astype(F32)
        vf = v_ref[0, hh].astype(F32).reshape(n, c, d)
        kk = _bdot('ncd,nsd->ncs', kb, kb)
        qk = _bdot('ncd,nsd->ncs', qb, kb)
        for di in range(2):
            col = lambda idx: jnp.sum(jnp.where(lane == idx * HEADS + h, gt, 0.0), axis=-1,
                                      keepdims=True).reshape(n, c, 1)
            beta = col(2 * di)
            gcol = col(2 * di + 1)
            grow = grow_ref[0, 2 * di + 1, hh]
            incl = (ii >= jj) if di == 0 else (ii <= jj)
            strict = (ii > jj) if di == 0 else (ii < jj)
            decay = jnp.where(incl, jnp.exp(jnp.where(incl, gcol - grow, 0.0)), 0.0)
            a = jnp.where(strict, beta * kk * decay, 0.0)
            tinv = _unit_tri_inverse(a, c)
            egc = jnp.exp(gcol)
            rhs = jnp.concatenate([vf * beta, kf * (beta * egc)], axis=-1).astype(BF16)
            uw = _bdot('ncs,nsd->ncd', tinv.astype(BF16), rhs).astype(BF16)
            iw = _bdot('ncs,nsd->ncd', (qk * decay).astype(BF16), uw)
            qp = qf * egc - iw[..., d:]
            glast = gcol[:, c - 1:c, :] if di == 0 else gcol[:, 0:1, :]
            kt = (kf * jnp.exp(glast - gcol)).astype(BF16)
            ktuw = _bdot('ncd,nce->nde', kt, uw)
            m_scr[di][hh] = (jnp.exp(glast) * eye_d - ktuw[..., d:]).astype(BF16)
            n_scr[di][hh] = ktuw[..., :d]
            qp_scr[di][hh] = qp.reshape(t, d).astype(BF16)
            if di == 0:
                oacc_scr[hh] = iw[..., :d].reshape(t, d)
            else:
                oacc_scr[hh] += iw[..., :d].reshape(t, d)
        return carry

    lax.fori_loop(0, hb, precompute, 0)

    for hh in range(hb):
        sf_scr[hh, 0] = jnp.zeros((d, d), BF16)
        sb_scr[hh, n - 1] = jnp.zeros((d, d), BF16)

    def scan(i, carry):
        nb = n - 1 - i
        ins = [(mf_scr[hh, i], sf_scr[hh, i], nf_scr[hh, i], mb_scr[hh, nb], sb_scr[hh, nb], nb_scr[hh, nb])
               for hh in range(hb)]
        outs = [((_dot(mf, sf) + nf).astype(BF16), (_dot(mb, sb) + nbv).astype(BF16))
                for mf, sf, nf, mb, sb, nbv in ins]
        for hh, (of, ob) in enumerate(outs):
            sf_scr[hh, i + 1] = of
            sb_scr[hh, nb - 1] = ob
        return carry

    lax.fori_loop(0, n - 1, scan, 0)

    def finish(hh, carry):
        qp = jnp.concatenate([qpf_scr[hh], qpb_scr[hh]], axis=-1).reshape(n, c, 2 * d)
        st = jnp.concatenate([sf_scr[hh], sb_scr[hh]], axis=-2)
        o = oacc_scr[hh] + _bdot('nck,nkd->ncd', qp, st).reshape(t, d)
        o_ref[0, hh] = (_rms(o, nw_ref[...]) * _silu(z_ref[0, hh])).astype(BF16)
        return carry

    lax.fori_loop(0, hb, finish, 0)


def _gdn_core(q, k, v, z, g, grow, norm_w):
    bsz, _, t, d = q.shape
    c, hb = GDN_CHUNK, GDN_HEADS_PER_STEP
    n = t // c
    grow = grow.reshape(bsz, 4, HEADS, n, 1, c)
    head = pl.BlockSpec((1, hb, t, d), lambda b, h: (b, h, 0, 0))
    mats = lambda dt: pltpu.VMEM((hb, n, d, d), dt)
    return pl.pallas_call(
        functools.partial(_gdn_core_kernel, t=t, c=c, hb=hb),
        grid=(bsz, HEADS // hb),
        in_specs=[head, head, head, head,
                  pl.BlockSpec((1, t, LANES), lambda b, h: (b, 0, 0)),
                  pl.BlockSpec((1, 4, hb, n, 1, c), lambda b, h: (b, 0, h, 0, 0, 0)),
                  pl.BlockSpec((1, d), lambda b, h: (0, 0))],
        out_specs=head,
        out_shape=jax.ShapeDtypeStruct((bsz, HEADS, t, d), BF16),
        scratch_shapes=[mats(BF16), mats(BF16), mats(F32), mats(F32),
                        pltpu.VMEM((hb, t, d), BF16), pltpu.VMEM((hb, t, d), BF16),
                        mats(BF16), mats(BF16), pltpu.VMEM((hb, t, d), F32)],
        compiler_params=_params("parallel", "parallel"),
        name="gdn_core",
    )(q, k, v, z, g, grow, norm_w[None])


def _router_epilogue(x1, nffn_ref, wr_ref, hn_ref, lgt_ref):
    hn = _rms(x1, nffn_ref[...])
    hi = hn.astype(BF16)
    lo = (hn - hi.astype(F32)).astype(BF16)
    _to_token_major(hn_ref.at[0], hn)
    lg = _dot(hi, wr_ref[0]) + _dot(hi, wr_ref[1]) + _dot(lo, wr_ref[0])
    lgt_ref[0] = lg.T[:N_EXPERTS]


def _router_weights(w_router):
    wp = jnp.pad(w_router, ((0, 0), (0, LANES - N_EXPERTS)))
    hi = wp.astype(BF16)
    lo = (wp - hi.astype(F32)).astype(BF16)
    return jnp.stack([hi, lo])


def _outproj_kernel(og_ref, x_ref, wout_ref, nffn_ref, wr_ref, x1_ref, hn_ref, lgt_ref):
    og = jnp.concatenate([og_ref[0, h] for h in range(HEADS)], axis=-1)
    x1 = x_ref[0] + _dot(og, wout_ref[...])
    x1_ref[0] = x1
    _router_epilogue(x1, nffn_ref, wr_ref, hn_ref, lgt_ref)


def _router_out(bsz, t, d, tm):
    rows = d // LANES
    specs = [pl.BlockSpec((1, tm, d), lambda b, tt: (b, tt, 0)),
             pl.BlockSpec((1, tm * rows, LANES), lambda b, tt: (b, tt, 0)),
             pl.BlockSpec((1, N_EXPERTS, tm), lambda b, tt: (b, 0, tt))]
    shapes = [jax.ShapeDtypeStruct((bsz, t, d), F32), jax.ShapeDtypeStruct((bsz, t * rows, LANES), F32),
              jax.ShapeDtypeStruct((bsz, N_EXPERTS, t), F32)]
    return specs, shapes


def _outproj_router(og, x, w_out, norm_ffn, w_router, *, tm=512):
    bsz, t, d = x.shape
    full = lambda shape: pl.BlockSpec(shape, lambda b, tt: (0,) * len(shape))
    out_specs, out_shape = _router_out(bsz, t, d, tm)
    return pl.pallas_call(
        _outproj_kernel,
        grid=(bsz, t // tm),
        in_specs=[pl.BlockSpec((1, HEADS, tm, HEAD_DIM), lambda b, tt: (b, 0, tt, 0)),
                  pl.BlockSpec((1, tm, d), lambda b, tt: (b, tt, 0)),
                  full(w_out.shape), full((1, d)), full((2, d, LANES))],
        out_specs=out_specs,
        out_shape=out_shape,
        compiler_params=_params("parallel", "parallel"),
        name="outproj_router",
    )(og, x, w_out.astype(BF16), norm_ffn[None], _router_weights(w_router))


def _conformer_kernel(xp_ref, xc_ref, xn_ref, nmix_ref, w1_ref, b1_ref, dww_ref, dwb_ref, lng_ref, lnb_ref,
                      w2_ref, b2_ref, nffn_ref, wr_ref, x1_ref, hn_ref, lgt_ref, u_scr, c_scr,
                      *, tm, nt, halo):
    d = xc_ref.shape[-1]
    x_ext = jnp.concatenate([xp_ref[0], xc_ref[0], xn_ref[0]], axis=0)
    hn = _rms(x_ext, nmix_ref[...]).astype(BF16)
    pg = _dot(hn, w1_ref[...]) + b1_ref[...]
    u = pg[:, :d] * _sigmoid(pg[:, d:])
    _to_token_major(u_scr, jnp.where(_ext_valid(tm, halo, nt), u, 0.0))
    _depthwise_conv_blocks(u_scr, c_scr, lambda j: dww_ref[j], CF_CONV, halo - CF_CONV // 2, tm, lambda a: a)
    conv = jnp.concatenate(_from_token_major(c_scr, tm), axis=-1) + dwb_ref[...]
    xc = conv - jnp.mean(conv, axis=-1, keepdims=True)
    y = xc * lax.rsqrt(jnp.mean(xc * xc, axis=-1, keepdims=True) + EPS) * lng_ref[...] + lnb_ref[...]
    mix = _dot(_silu(y).astype(BF16), w2_ref[...]) + b2_ref[...]
    x1 = xc_ref[0] + mix
    x1_ref[0] = x1
    _router_epilogue(x1, nffn_ref, wr_ref, hn_ref, lgt_ref)


def _conformer_router(x, norm_mix, w_pw1, b_pw1, dw_w, dw_b, ln_g, ln_b, w_pw2, b_pw2, norm_ffn, w_router,
                      *, tm=256, halo=16):
    bsz, t, d = x.shape
    nt = t // tm
    prev, cur, nxt = _halo_specs(tm, halo, t // halo, d)
    full = lambda shape: pl.BlockSpec(shape, lambda b, tt: (0,) * len(shape))
    dww = jnp.pad(dw_w, ((0, 32 - CF_CONV), (0, 0))).reshape(32, SUBLANES, LANES)
    out_specs, out_shape = _router_out(bsz, t, d, tm)
    return pl.pallas_call(
        functools.partial(_conformer_kernel, tm=tm, nt=nt, halo=halo),
        grid=(bsz, nt),
        in_specs=[prev, cur, nxt, full((1, d)), full((d, 2 * d)), full((1, 2 * d)),
                  full((32, SUBLANES, LANES)), full((1, d)), full((1, d)), full((1, d)), full((d, d)),
                  full((1, d)), full((1, d)), full((2, d, LANES))],
        out_specs=out_specs,
        out_shape=out_shape,
        scratch_shapes=[pltpu.VMEM(((tm + 2 * halo) * SUBLANES, LANES), F32),
                        pltpu.VMEM((tm * SUBLANES, LANES), F32)],
        compiler_params=_params("parallel", "parallel"),
        name="conformer_router",
    )(x, x, x, norm_mix[None], w_pw1.astype(BF16), b_pw1[None], dww, dw_b[None], ln_g[None], ln_b[None],
      w_pw2.astype(BF16), b_pw2[None], norm_ffn[None], _router_weights(w_router))


def _route_kernel(lg_ref, row_ref, pidx_ref, p_ref, tri_scr, *, t, cap):
    @pl.when(pl.program_id(0) == 0)
    def _():
        r = lax.broadcasted_iota(jnp.int32, (t, t), 0)
        cidx = lax.broadcasted_iota(jnp.int32, (t, t), 1)
        tri_scr[...] = jnp.where(r < cidx, 1.0, 0.0).astype(BF16)

    count = lambda m: jnp.sum(jnp.where(m, 1.0, 0.0), axis=1, keepdims=True)
    trow = lax.broadcasted_iota(jnp.int32, (1, t), 1).astype(F32)
    slot_sub = lax.broadcasted_iota(jnp.int32, (cap, 1), 0)
    lane = lax.broadcasted_iota(jnp.int32, (1, LANES), 1)
    for bi in range(lg_ref.shape[0]):
        lt = lg_ref[bi]
        ne = lt.shape[0]
        e = jnp.exp(lt - jnp.max(lt, axis=0, keepdims=True))
        p = e / jnp.sum(e, axis=0, keepdims=True)
        p_ref[bi] = p
        bits = pltpu.bitcast(p, jnp.int32)
        thr = jnp.zeros((ne, 1), jnp.int32)
        for bit in range(30, -1, -1):
            cand = thr | (1 << bit)
            thr = jnp.where(count(bits >= cand) >= cap, cand, thr)
        gt = bits > thr
        eq = bits == thr
        need = cap - count(gt)
        eq_before = _dot(jnp.where(eq, 1.0, 0.0).astype(BF16), tri_scr[...])
        sel = gt | (eq & (eq_before < need))
        slot = _dot(jnp.where(sel, 1.0, 0.0).astype(BF16), tri_scr[...])
        pos = jnp.where(sel, slot.astype(jnp.int32), -1)
        tokm = jnp.zeros((cap, LANES), F32)
        for ei in range(ne):
            match = slot_sub == pos[ei:ei + 1, :]
            tokm = tokm + jnp.where(lane == ei, jnp.sum(jnp.where(match, trow, 0.0), axis=1, keepdims=True), 0.0)
        tok = tokm.T[:ne].astype(jnp.int32)
        row_ref[bi] = tok * SUBLANES
        pidx_ref[bi] = tok + t * lax.broadcasted_iota(jnp.int32, (ne, 1), 0)


def _route(logits_t, cap, *, nb=4):
    bsz, e, t = logits_t.shape
    return pl.pallas_call(
        functools.partial(_route_kernel, t=t, cap=cap),
        grid=(bsz // nb,),
        in_specs=[pl.BlockSpec((nb, e, t), lambda b: (b, 0, 0))],
        out_specs=[pl.BlockSpec((nb, e, cap), lambda b: (b, 0, 0)), pl.BlockSpec((nb, e, cap), lambda b: (b, 0, 0)),
                   pl.BlockSpec((nb, e, t), lambda b: (b, 0, 0))],
        out_shape=[jax.ShapeDtypeStruct((bsz, e, cap), jnp.int32), jax.ShapeDtypeStruct((bsz, e, cap), jnp.int32),
                   jax.ShapeDtypeStruct((bsz, e, t), F32)],
        scratch_shapes=[pltpu.VMEM((t, t), BF16)],
        compiler_params=_params("arbitrary"),
        name="route",
    )(logits_t)


def _gather_kernel(row_ref, hn_ref, x_ref, rows_scr, *, cap, ne):
    def per_expert(e, carry):
        for s in range(cap):
            start = pl.multiple_of(row_ref[0, 0, e * cap + s], SUBLANES)
            rows_scr[pl.ds(s * SUBLANES, SUBLANES), :] = hn_ref[0, pl.ds(start, SUBLANES), :]
        x_ref[e, 0] = jnp.concatenate(_from_token_major(rows_scr, cap), axis=-1).astype(BF16)
        return carry

    lax.fori_loop(0, ne, per_expert, 0)


def _gather(hn_tm, row, cap):
    bsz, rows, _ = hn_tm.shape
    e = row.shape[1]
    d = SUBLANES * LANES
    return pl.pallas_call(
        functools.partial(_gather_kernel, cap=cap, ne=e),
        grid=(bsz,),
        in_specs=[pl.BlockSpec((1, 1, e * cap), lambda b: (b, 0, 0), memory_space=pltpu.SMEM),
                  pl.BlockSpec((1, rows, LANES), lambda b: (b, 0, 0))],
        out_specs=pl.BlockSpec((e, 1, cap, d), lambda b: (0, b, 0, 0)),
        out_shape=jax.ShapeDtypeStruct((e, bsz, cap, d), BF16),
        scratch_shapes=[pltpu.VMEM((cap * SUBLANES, LANES), F32)],
        compiler_params=_params("parallel"),
        name="gather",
    )(row.reshape(bsz, 1, e * cap), hn_tm)


def _expert_kernel(x_ref, wg_ref, wu_ref, wd_ref, y_ref, wgb, wub, wdb):
    @pl.when(pl.program_id(1) == 0)
    def _():
        wgb[...] = wg_ref[0, 0].astype(BF16)
        wub[...] = wu_ref[0, 0].astype(BF16)
        wdb[...] = wd_ref[0, 0].astype(BF16)

    _, nb, cap, d = x_ref.shape
    xin = x_ref[0].reshape(nb * cap, d)
    hid = (_silu(_dot(xin, wgb[...])) * _dot(xin, wub[...])).astype(BF16)
    y = _dot(hid, wdb[...])
    half = cap // 2
    for bi in range(nb):
        hi = pltpu.bitcast(y[bi * cap:bi * cap + half].astype(BF16).astype(F32), jnp.int32)
        lo = pltpu.bitcast(y[bi * cap + half:(bi + 1) * cap].astype(BF16).astype(F32), jnp.int32)
        _to_token_major(y_ref.at[0, bi], hi | lax.shift_right_logical(lo, 16))


def _experts(xin, w_gate, w_up, w_down, layer, *, nb=2):
    e, bsz, cap, d = xin.shape
    ff = w_gate.shape[-1]
    return pl.pallas_call(
        _expert_kernel,
        grid=(e, bsz // nb),
        in_specs=[pl.BlockSpec((1, nb, cap, d), lambda ei, b: (ei, b, 0, 0)),
                  pl.BlockSpec((1, 1, d, ff), lambda ei, b: (layer, ei, 0, 0)),
                  pl.BlockSpec((1, 1, d, ff), lambda ei, b: (layer, ei, 0, 0)),
                  pl.BlockSpec((1, 1, ff, d), lambda ei, b: (layer, ei, 0, 0))],
        out_specs=pl.BlockSpec((1, nb, cap // 2 * SUBLANES, LANES), lambda ei, b: (ei, b, 0, 0)),
        out_shape=jax.ShapeDtypeStruct((e, bsz, cap // 2 * SUBLANES, LANES), jnp.int32),
        scratch_shapes=[pltpu.VMEM((d, ff), BF16), pltpu.VMEM((d, ff), BF16), pltpu.VMEM((ff, d), BF16)],
        compiler_params=_params("arbitrary", "arbitrary"),
        name="experts",
    )(xin, w_gate, w_up, w_down)


SCATTER_UNROLL = 8


def _scatter_kernel(row_ref, pidx_ref, p_ref, x_ref, y_ref, fw_ref, o_ref, acc_scr, *, cap, t, eg, final):
    g = pl.program_id(1)

    @pl.when(g == 0)
    def _():
        _to_token_major(acc_scr, x_ref[0])

    def per_expert(el, carry):
        e = g * eg + el
        half = cap // 2
        for s0 in range(0, half, SCATTER_UNROLL // 2):
            rows = []
            for s in range(s0, s0 + SCATTER_UNROLL // 2):
                word = y_ref[el, 0, pl.ds(s * SUBLANES, SUBLANES), :]
                pair = (pltpu.bitcast(word & -65536, F32), pltpu.bitcast(word << 16, F32))
                for slot, yrow in zip((s, s + half), pair):
                    aff = p_ref[0, 0, pidx_ref[0, 0, e * cap + slot]]
                    start = pl.multiple_of(row_ref[0, 0, e * cap + slot], SUBLANES)
                    rows.append((start, acc_scr[pl.ds(start, SUBLANES), :] + aff * yrow))
            for start, val in rows:
                acc_scr[pl.ds(start, SUBLANES), :] = val
        return carry

    lax.fori_loop(0, eg, per_expert, 0)

    @pl.when(g == pl.num_programs(1) - 1)
    def _():
        out = jnp.concatenate(_from_token_major(acc_scr, t), axis=-1)
        o_ref[0] = _rms(out, fw_ref[...]) if final else out


def _scatter(x, row, pidx, p, y, final_w, cap, *, final, eg=2):
    bsz, t, d = x.shape
    e = y.shape[0]
    smem = lambda n: pl.BlockSpec((1, 1, n), lambda b, g: (b, 0, 0), memory_space=pltpu.SMEM)
    return pl.pallas_call(
        functools.partial(_scatter_kernel, cap=cap, t=t, eg=eg, final=final),
        grid=(bsz, e // eg),
        in_specs=[smem(e * cap), smem(e * cap), smem(e * t),
                  pl.BlockSpec((1, t, d), lambda b, g: (b, 0, 0)),
                  pl.BlockSpec((eg, 1, cap // 2 * SUBLANES, LANES), lambda b, g: (g, b, 0, 0)),
                  pl.BlockSpec((1, d), lambda b, g: (0, 0))],
        out_specs=pl.BlockSpec((1, t, d), lambda b, g: (b, 0, 0)),
        out_shape=jax.ShapeDtypeStruct((bsz, t, d), F32),
        scratch_shapes=[pltpu.VMEM((t * SUBLANES, LANES), F32)],
        compiler_params=_params("parallel", "arbitrary"),
        name="scatter",
    )(row.reshape(bsz, 1, e * cap), pidx.reshape(bsz, 1, e * cap), p.reshape(bsz, 1, e * t), x, y, final_w[None])


def _moe(x1, hn, logits_t, w_gate, w_up, w_down, layer, final_w, *, final):
    t = x1.shape[1]
    cap = CAPACITY_FACTOR * t // N_EXPERTS
    row, pidx, p = _route(logits_t, cap)
    y = _experts(_gather(hn, row, cap), w_gate, w_up, w_down, layer)
    return _scatter(x1, row, pidx, p, y, final_w, cap, final=final)


def kernel(x, norm_mix, norm_ffn, dn_w_in, dn_conv_w, dn_a_log, dn_dt_bias, dn_norm_w, dn_w_out, cf_w_pw1, cf_b_pw1, cf_dw_w, cf_dw_b, cf_ln_g, cf_ln_b, cf_w_pw2, cf_b_pw2, moe_w_router, moe_w_gate, moe_w_up, moe_w_down, final_norm):
    q, k, v, z, g, grow = _gdn_pre(x, norm_mix[0], dn_w_in[0], dn_conv_w[0], dn_a_log[0], dn_dt_bias[0])
    og = _gdn_core(q, k, v, z, g, grow, dn_norm_w[0])
    x1, hn, lgt = _outproj_router(og, x, dn_w_out[0], norm_ffn[0], moe_w_router[0])
    x2 = _moe(x1, hn, lgt, moe_w_gate, moe_w_up, moe_w_down, 0, final_norm, final=False)
    x3, hn, lgt = _conformer_router(x2, norm_mix[1], cf_w_pw1[0], cf_b_pw1[0], cf_dw_w[0], cf_dw_b[0],
                                    cf_ln_g[0], cf_ln_b[0], cf_w_pw2[0], cf_b_pw2[0], norm_ffn[1],
                                    moe_w_router[1])
    return _moe(x3, hn, lgt, moe_w_gate, moe_w_up, moe_w_down, 1, final_norm, final=True)
```

```python
import functools

import jax
import jax.numpy as jnp
from jax import lax
from jax.experimental import pallas as pl
from jax.experimental.pallas import tpu as pltpu

F32 = jnp.float32
BF16 = jnp.bfloat16
EPS = 1e-6

LANES = 128
SUBLANES = 8
HEADS = 8
HEAD_DIM = 128
DN_CONV = 5
CF_CONV = 31
N_EXPERTS = 16
CAPACITY_FACTOR = 2
GDN_CHUNK = 128
GDN_HEADS_PER_STEP = 2
CONV_BLOCK = 16
VMEM_LIMIT = 56 * 1024 * 1024


def _params(*sem):
    return pltpu.CompilerParams(dimension_semantics=sem, vmem_limit_bytes=VMEM_LIMIT)


def _sigmoid(x):
    return 1.0 / (1.0 + jnp.exp(-x))


def _silu(x):
    return x * _sigmoid(x)


def _softplus(x):
    return jnp.maximum(x, 0.0) + jnp.log1p(jnp.exp(-jnp.abs(x)))


def _rms(x, w):
    return x * lax.rsqrt(jnp.mean(x * x, axis=-1, keepdims=True) + EPS) * w


def _dot(a, b):
    return jnp.dot(a, b, preferred_element_type=F32)


def _bdot(spec, a, b):
    return jnp.einsum(spec, a, b, preferred_element_type=F32)


def _seg_cumsum(x, seg, reverse):
    n = x.shape[0]
    r = lax.broadcasted_iota(jnp.int32, (n, 1), 0) % seg
    s = 1
    while s < seg:
        if reverse:
            x = x + jnp.where(r < seg - s, pltpu.roll(x, n - s, axis=0), 0.0)
        else:
            x = x + jnp.where(r >= s, pltpu.roll(x, s, axis=0), 0.0)
        s *= 2
    return x


def _halo_specs(tm, halo, nt_total, d):
    per = tm // halo
    prev = pl.BlockSpec((1, halo, d), lambda b, t: (b, jnp.maximum(t * per - 1, 0), 0))
    cur = pl.BlockSpec((1, tm, d), lambda b, t: (b, t, 0))
    nxt = pl.BlockSpec((1, halo, d), lambda b, t: (b, jnp.minimum((t + 1) * per, nt_total - 1), 0))
    return prev, cur, nxt


def _ext_valid(tm, halo, nt):
    t = pl.program_id(1)
    rows = lax.broadcasted_iota(jnp.int32, (tm + 2 * halo, 1), 0)
    lo = jnp.where(t == 0, halo, 0)
    hi = jnp.where(t == nt - 1, tm + halo, tm + 2 * halo)
    return (rows >= lo) & (rows < hi)


def _to_token_major(dst_ref, x, row0=0):
    rows = x.shape[0]
    for j in range(SUBLANES):
        dst_ref[pl.ds(row0 * SUBLANES + j, rows, stride=SUBLANES), :] = x[:, j * LANES:(j + 1) * LANES]


def _from_token_major(src_ref, rows):
    return [src_ref[pl.ds(j, rows, stride=SUBLANES), :] for j in range(SUBLANES)]


def _depthwise_conv_blocks(src_ref, dst_ref, tap, n_taps, first_row, tm, post):
    rows = CONV_BLOCK * SUBLANES
    for i in range(tm // CONV_BLOCK):
        acc = jnp.zeros((CONV_BLOCK, SUBLANES, LANES), F32)
        for j in range(n_taps):
            start = (i * CONV_BLOCK + first_row + j) * SUBLANES
            seg = src_ref[pl.ds(start, rows), :].reshape(CONV_BLOCK, SUBLANES, LANES)
            acc = acc + seg * tap(j)[None]
        dst_ref[pl.ds(i * rows, rows), :] = post(acc.reshape(rows, LANES))


def _gdn_pre_kernel(xp_ref, xc_ref, xn_ref, nw_ref, wqkvz_ref, wg_ref, cw_ref, alog_ref, dtb_ref,
                    q_ref, k_ref, v_ref, z_ref, g_ref, grow_ref, ext_ref, act_ref,
                    *, tm, nt, halo, chunk):
    width = HEADS * HEAD_DIM
    x_ext = jnp.concatenate([xp_ref[0], xc_ref[0], xn_ref[0]], axis=0)
    hn = _rms(x_ext, nw_ref[...]).astype(BF16)
    proj = _dot(hn, wqkvz_ref[...])
    valid = _ext_valid(tm, halo, nt)
    for h in range(HEADS):
        z_ref[0, h] = proj[halo:halo + tm, 3 * width + h * HEAD_DIM: 3 * width + (h + 1) * HEAD_DIM]

    l2 = lambda a: a * lax.rsqrt(jnp.sum(a * a, axis=-1, keepdims=True) + EPS)
    posts = (lambda a: l2(a) * (HEAD_DIM ** -0.5), l2, lambda a: a)
    for grp, out_ref in enumerate((q_ref, k_ref, v_ref)):
        _to_token_major(ext_ref, jnp.where(valid, proj[:, grp * width:(grp + 1) * width], 0.0))
        _depthwise_conv_blocks(ext_ref, act_ref, lambda j: cw_ref[j, grp], DN_CONV,
                               halo - DN_CONV // 2, tm, _silu)
        for h, tile in enumerate(_from_token_major(act_ref, tm)):
            out_ref[0, h] = posts[grp](tile).astype(BF16)

    graw = _dot(hn[halo:halo + tm], wg_ref[...])
    lane = lax.broadcasted_iota(jnp.int32, (1, LANES), 1)
    beta = _sigmoid(graw)
    logg = -jnp.exp(alog_ref[...]) * _softplus(graw + dtb_ref[...])
    cum_f = _seg_cumsum(logg, chunk, reverse=False)
    cum_b = _seg_cumsum(logg, chunk, reverse=True)
    is_beta = ((lane % (2 * HEADS)) < HEADS) & (lane < 4 * HEADS)
    is_f = (lane >= HEADS) & (lane < 2 * HEADS)
    is_b = (lane >= 3 * HEADS) & (lane < 4 * HEADS)
    g = jnp.where(is_beta, beta, jnp.where(is_f, cum_f, jnp.where(is_b, cum_b, 0.0)))
    g_ref[0] = g
    grow_ref[0] = g.T[:4 * HEADS]


def _gdn_pre(x, norm_w, w_in, conv_w, a_log, dt_bias, *, tm=256, halo=8):
    bsz, t, d = x.shape
    width = HEADS * HEAD_DIM
    nt = t // tm
    wqkvz = w_in[:, :4 * width].astype(BF16)
    wg = jnp.pad(w_in[:, 4 * width:], ((0, 0), (0, LANES - 4 * HEADS))).astype(BF16)
    cw = jnp.pad(conv_w, ((0, 8 - DN_CONV), (0, 0))).reshape(8, 3, HEADS, HEAD_DIM)
    zeros = jnp.zeros((HEADS,), F32)
    alog = jnp.pad(jnp.concatenate([zeros, a_log[0], zeros, a_log[1]]), (0, LANES - 4 * HEADS))[None]
    dtb = jnp.pad(jnp.concatenate([zeros, dt_bias[0], zeros, dt_bias[1]]), (0, LANES - 4 * HEADS))[None]
    prev, cur, nxt = _halo_specs(tm, halo, t // halo, d)
    full = lambda shape: pl.BlockSpec(shape, lambda b, tt: (0,) * len(shape))
    heads = pl.BlockSpec((1, HEADS, tm, HEAD_DIM), lambda b, tt: (b, 0, tt, 0))
    hshape = (bsz, HEADS, t, HEAD_DIM)
    return pl.pallas_call(
        functools.partial(_gdn_pre_kernel, tm=tm, nt=nt, halo=halo, chunk=GDN_CHUNK),
        grid=(bsz, nt),
        in_specs=[prev, cur, nxt, full((1, d)), full((d, 4 * width)), full((d, LANES)),
                  full((8, 3, HEADS, HEAD_DIM)), full((1, LANES)), full((1, LANES))],
        out_specs=[heads, heads, heads, heads,
                   pl.BlockSpec((1, tm, LANES), lambda b, tt: (b, tt, 0)),
                   pl.BlockSpec((1, 4 * HEADS, tm), lambda b, tt: (b, 0, tt))],
        out_shape=[jax.ShapeDtypeStruct(hshape, BF16)] * 3
        + [jax.ShapeDtypeStruct(hshape, F32), jax.ShapeDtypeStruct((bsz, t, LANES), F32),
           jax.ShapeDtypeStruct((bsz, 4 * HEADS, t), F32)],
        scratch_shapes=[pltpu.VMEM(((tm + 2 * halo) * SUBLANES, LANES), F32),
                        pltpu.VMEM((tm * SUBLANES, LANES), F32)],
        compiler_params=_params("parallel", "parallel"),
        name="gdn_pre",
    )(x, x, x, norm_w[None], wqkvz, wg, cw, alog, dtb)


def _unit_tri_inverse(a, c):
    ii = lax.broadcasted_iota(jnp.int32, (c, c), 0)
    jj = lax.broadcasted_iota(jnp.int32, (c, c), 1)
    eye = (ii == jj).astype(F32)
    x = eye - jnp.where((ii // 2) == (jj // 2), a, 0.0)
    s = 2
    while s < c:
        off = ((ii // (2 * s)) == (jj // (2 * s))) & ((ii // s) != (jj // s))
        a_l = jnp.where(off, a, 0.0).astype(BF16)
        y = _bdot('nij,njk->nik', x.astype(BF16), a_l)
        x = x - _bdot('nij,njk->nik', y.astype(BF16), x.astype(BF16))
        s *= 2
    return x


def _gdn_core_kernel(q_ref, k_ref, v_ref, z_ref, g_ref, grow_ref, nw_ref, o_ref,
                     mf_scr, mb_scr, nf_scr, nb_scr, qpf_scr, qpb_scr, sf_scr, sb_scr, oacc_scr,
                     *, t, c, hb):
    n = t // c
    d = HEAD_DIM
    lane = lax.broadcasted_iota(jnp.int32, (1, LANES), 1)
    ii = lax.broadcasted_iota(jnp.int32, (c, c), 0)
    jj = lax.broadcasted_iota(jnp.int32, (c, c), 1)
    eye_d = (lax.broadcasted_iota(jnp.int32, (d, d), 0) == lax.broadcasted_iota(jnp.int32, (d, d), 1)).astype(F32)
    m_scr, n_scr, qp_scr = (mf_scr, mb_scr), (nf_scr, nb_scr), (qpf_scr, qpb_scr)

    def precompute(hh, carry):
        h = pl.program_id(1) * hb + hh
        gt = g_ref[0]
        kb = k_ref[0, hh].reshape(n, c, d)
        qb = q_ref[0, hh].reshape(n, c, d)
        kf = kb.astype(F32)
        qf = qb.astype(F32)
        vf = v_ref[0, hh].astype(F32).reshape(n, c, d)
        kk = _bdot('ncd,nsd->ncs', kb, kb)
        qk = _bdot('ncd,nsd->ncs', qb, kb)
        for di in range(2):
            col = lambda idx: jnp.sum(jnp.where(lane == idx * HEADS + h, gt, 0.0), axis=-1,
                                      keepdims=True).reshape(n, c, 1)
            beta = col(2 * di)
            gcol = col(2 * di + 1)
            grow = grow_ref[0, 2 * di + 1, hh]
            incl = (ii >= jj) if di == 0 else (ii <= jj)
            strict = (ii > jj) if di == 0 else (ii < jj)
            decay = jnp.where(incl, jnp.exp(jnp.where(incl, gcol - grow, 0.0)), 0.0)
            a = jnp.where(strict, beta * kk * decay, 0.0)
            tinv = _unit_tri_inverse(a, c)
            egc = jnp.exp(gcol)
            rhs = jnp.concatenate([vf * beta, kf * (beta * egc)], axis=-1).astype(BF16)
            uw = _bdot('ncs,nsd->ncd', tinv.astype(BF16), rhs).astype(BF16)
            iw = _bdot('ncs,nsd->ncd', (qk * decay).astype(BF16), uw)
            qp = qf * egc - iw[..., d:]
            glast = gcol[:, c - 1:c, :] if di == 0 else gcol[:, 0:1, :]
            kt = (kf * jnp.exp(glast - gcol)).astype(BF16)
            ktuw = _bdot('ncd,nce->nde', kt, uw)
            m_scr[di][hh] = (jnp.exp(glast) * eye_d - ktuw[..., d:]).astype(BF16)
            n_scr[di][hh] = ktuw[..., :d]
            qp_scr[di][hh] = qp.reshape(t, d).astype(BF16)
            if di == 0:
                oacc_scr[hh] = iw[..., :d].reshape(t, d)
            else:
                oacc_scr[hh] += iw[..., :d].reshape(t, d)
        return carry

    lax.fori_loop(0, hb, precompute, 0)

    for hh in range(hb):
        sf_scr[hh, 0] = jnp.zeros((d, d), BF16)
        sb_scr[hh, n - 1] = jnp.zeros((d, d), BF16)

    def scan(i, carry):
        nb = n - 1 - i
        ins = [(mf_scr[hh, i], sf_scr[hh, i], nf_scr[hh, i], mb_scr[hh, nb], sb_scr[hh, nb], nb_scr[hh, nb])
               for hh in range(hb)]
        outs = [((_dot(mf, sf) + nf).astype(BF16), (_dot(mb, sb) + nbv).astype(BF16))
                for mf, sf, nf, mb, sb, nbv in ins]
        for hh, (of, ob) in enumerate(outs):
            sf_scr[hh, i + 1] = of
            sb_scr[hh, nb - 1] = ob
        return carry

    lax.fori_loop(0, n - 1, scan, 0)

    def finish(hh, carry):
        qp = jnp.concatenate([qpf_scr[hh], qpb_scr[hh]], axis=-1).reshape(n, c, 2 * d)
        st = jnp.concatenate([sf_scr[hh], sb_scr[hh]], axis=-2)
        o = oacc_scr[hh] + _bdot('nck,nkd->ncd', qp, st).reshape(t, d)
        o_ref[0, hh] = (_rms(o, nw_ref[...]) * _silu(z_ref[0, hh])).astype(BF16)
        return carry

    lax.fori_loop(0, hb, finish, 0)


def _gdn_core(q, k, v, z, g, grow, norm_w):
    bsz, _, t, d = q.shape
    c, hb = GDN_CHUNK, GDN_HEADS_PER_STEP
    n = t // c
    grow = grow.reshape(bsz, 4, HEADS, n, 1, c)
    head = pl.BlockSpec((1, hb, t, d), lambda b, h: (b, h, 0, 0))
    mats = lambda dt: pltpu.VMEM((hb, n, d, d), dt)
    return pl.pallas_call(
        functools.partial(_gdn_core_kernel, t=t, c=c, hb=hb),
        grid=(bsz, HEADS // hb),
        in_specs=[head, head, head, head,
                  pl.BlockSpec((1, t, LANES), lambda b, h: (b, 0, 0)),
                  pl.BlockSpec((1, 4, hb, n, 1, c), lambda b, h: (b, 0, h, 0, 0, 0)),
                  pl.BlockSpec((1, d), lambda b, h: (0, 0))],
        out_specs=head,
        out_shape=jax.ShapeDtypeStruct((bsz, HEADS, t, d), BF16),
        scratch_shapes=[mats(BF16), mats(BF16), mats(F32), mats(F32),
                        pltpu.VMEM((hb, t, d), BF16), pltpu.VMEM((hb, t, d), BF16),
                        mats(BF16), mats(BF16), pltpu.VMEM((hb, t, d), F32)],
        compiler_params=_params("parallel", "parallel"),
        name="gdn_core",
    )(q, k, v, z, g, grow, norm_w[None])


def _router_epilogue(x1, r0, nffn_ref, wr_ref, hn_ref, lgt_ref):
    hn = _rms(x1, nffn_ref[...])
    hi = hn.astype(BF16)
    lo = (hn - hi.astype(F32)).astype(BF16)
    _to_token_major(hn_ref.at[0], hn, r0)
    lg = _dot(hi, wr_ref[0]) + _dot(hi, wr_ref[1]) + _dot(lo, wr_ref[0])
    lgt_ref[0, :, pl.ds(r0, x1.shape[0])] = lg.T[:N_EXPERTS]


def _router_weights(w_router):
    wp = jnp.pad(w_router, ((0, 0), (0, LANES - N_EXPERTS)))
    hi = wp.astype(BF16)
    lo = (wp - hi.astype(F32)).astype(BF16)
    return jnp.stack([hi, lo])


OUTPROJ_ROW_BLOCK = 256


def _outproj_kernel(og_ref, x_ref, wout_ref, nffn_ref, wr_ref, x1_ref, hn_ref, lgt_ref):
    rb = OUTPROJ_ROW_BLOCK
    for r0 in range(0, x_ref.shape[1], rb):
        og = jnp.concatenate([og_ref[0, h, pl.ds(r0, rb), :] for h in range(HEADS)], axis=-1)
        x1 = x_ref[0, pl.ds(r0, rb), :] + _dot(og, wout_ref[...])
        x1_ref[0, pl.ds(r0, rb), :] = x1
        _router_epilogue(x1, r0, nffn_ref, wr_ref, hn_ref, lgt_ref)


def _router_out(bsz, t, d, tm):
    rows = d // LANES
    specs = [pl.BlockSpec((1, tm, d), lambda b, tt: (b, tt, 0)),
             pl.BlockSpec((1, tm * rows, LANES), lambda b, tt: (b, tt, 0)),
             pl.BlockSpec((1, N_EXPERTS, tm), lambda b, tt: (b, 0, tt))]
    shapes = [jax.ShapeDtypeStruct((bsz, t, d), F32), jax.ShapeDtypeStruct((bsz, t * rows, LANES), F32),
              jax.ShapeDtypeStruct((bsz, N_EXPERTS, t), F32)]
    return specs, shapes


def _outproj_router(og, x, w_out, norm_ffn, w_router, *, tm=512):
    bsz, t, d = x.shape
    full = lambda shape: pl.BlockSpec(shape, lambda b, tt: (0,) * len(shape))
    out_specs, out_shape = _router_out(bsz, t, d, tm)
    return pl.pallas_call(
        _outproj_kernel,
        grid=(bsz, t // tm),
        in_specs=[pl.BlockSpec((1, HEADS, tm, HEAD_DIM), lambda b, tt: (b, 0, tt, 0)),
                  pl.BlockSpec((1, tm, d), lambda b, tt: (b, tt, 0)),
                  full(w_out.shape), full((1, d)), full((2, d, LANES))],
        out_specs=out_specs,
        out_shape=out_shape,
        compiler_params=_params("parallel", "parallel"),
        name="outproj_router",
    )(og, x, w_out.astype(BF16), norm_ffn[None], _router_weights(w_router))


def _conformer_kernel(xp_ref, xc_ref, xn_ref, nmix_ref, w1_ref, b1_ref, dww_ref, dwb_ref, lng_ref, lnb_ref,
                      w2_ref, b2_ref, nffn_ref, wr_ref, x1_ref, hn_ref, lgt_ref, u_scr, c_scr,
                      *, tm, nt, halo):
    d = xc_ref.shape[-1]
    x_ext = jnp.concatenate([xp_ref[0], xc_ref[0], xn_ref[0]], axis=0)
    hn = _rms(x_ext, nmix_ref[...]).astype(BF16)
    pg = _dot(hn, w1_ref[...]) + b1_ref[...]
    u = pg[:, :d] * _sigmoid(pg[:, d:])
    _to_token_major(u_scr, jnp.where(_ext_valid(tm, halo, nt), u, 0.0))
    _depthwise_conv_blocks(u_scr, c_scr, lambda j: dww_ref[j], CF_CONV, halo - CF_CONV // 2, tm, lambda a: a)
    conv = jnp.concatenate(_from_token_major(c_scr, tm), axis=-1) + dwb_ref[...]
    xc = conv - jnp.mean(conv, axis=-1, keepdims=True)
    y = xc * lax.rsqrt(jnp.mean(xc * xc, axis=-1, keepdims=True) + EPS) * lng_ref[...] + lnb_ref[...]
    mix = _dot(_silu(y).astype(BF16), w2_ref[...]) + b2_ref[...]
    x1 = xc_ref[0] + mix
    x1_ref[0] = x1
    _router_epilogue(x1, 0, nffn_ref, wr_ref, hn_ref, lgt_ref)


def _conformer_router(x, norm_mix, w_pw1, b_pw1, dw_w, dw_b, ln_g, ln_b, w_pw2, b_pw2, norm_ffn, w_router,
                      *, tm=256, halo=16):
    bsz, t, d = x.shape
    nt = t // tm
    prev, cur, nxt = _halo_specs(tm, halo, t // halo, d)
    full = lambda shape: pl.BlockSpec(shape, lambda b, tt: (0,) * len(shape))
    dww = jnp.pad(dw_w, ((0, 32 - CF_CONV), (0, 0))).reshape(32, SUBLANES, LANES)
    out_specs, out_shape = _router_out(bsz, t, d, tm)
    return pl.pallas_call(
        functools.partial(_conformer_kernel, tm=tm, nt=nt, halo=halo),
        grid=(bsz, nt),
        in_specs=[prev, cur, nxt, full((1, d)), full((d, 2 * d)), full((1, 2 * d)),
                  full((32, SUBLANES, LANES)), full((1, d)), full((1, d)), full((1, d)), full((d, d)),
                  full((1, d)), full((1, d)), full((2, d, LANES))],
        out_specs=out_specs,
        out_shape=out_shape,
        scratch_shapes=[pltpu.VMEM(((tm + 2 * halo) * SUBLANES, LANES), F32),
                        pltpu.VMEM((tm * SUBLANES, LANES), F32)],
        compiler_params=_params("parallel", "parallel"),
        name="conformer_router",
    )(x, x, x, norm_mix[None], w_pw1.astype(BF16), b_pw1[None], dww, dw_b[None], ln_g[None], ln_b[None],
      w_pw2.astype(BF16), b_pw2[None], norm_ffn[None], _router_weights(w_router))


def _route_kernel(lg_ref, row_ref, pidx_ref, p_ref, tri_scr, *, t, cap):
    @pl.when(pl.program_id(0) == 0)
    def _():
        r = lax.broadcasted_iota(jnp.int32, (t, t), 0)
        cidx = lax.broadcasted_iota(jnp.int32, (t, t), 1)
        tri_scr[...] = jnp.where(r < cidx, 1.0, 0.0).astype(BF16)

    count = lambda m: jnp.sum(jnp.where(m, 1.0, 0.0), axis=1, keepdims=True)
    trow = lax.broadcasted_iota(jnp.int32, (1, t), 1).astype(F32)
    slot_sub = lax.broadcasted_iota(jnp.int32, (cap, 1), 0)
    lane = lax.broadcasted_iota(jnp.int32, (1, LANES), 1)
    for bi in range(lg_ref.shape[0]):
        lt = lg_ref[bi]
        ne = lt.shape[0]
        e = jnp.exp(lt - jnp.max(lt, axis=0, keepdims=True))
        p = e / jnp.sum(e, axis=0, keepdims=True)
        p_ref[bi] = p
        bits = pltpu.bitcast(p, jnp.int32)
        thr = jnp.zeros((ne, 1), jnp.int32)
        for bit in range(30, -1, -1):
            cand = thr | (1 << bit)
            thr = jnp.where(count(bits >= cand) >= cap, cand, thr)
        gt = bits > thr
        eq = bits == thr
        need = cap - count(gt)
        eq_before = _dot(jnp.where(eq, 1.0, 0.0).astype(BF16), tri_scr[...])
        sel = gt | (eq & (eq_before < need))
        slot = _dot(jnp.where(sel, 1.0, 0.0).astype(BF16), tri_scr[...])
        pos = jnp.where(sel, slot.astype(jnp.int32), -1)
        tokm = jnp.zeros((cap, LANES), F32)
        for ei in range(ne):
            match = slot_sub == pos[ei:ei + 1, :]
            tokm = tokm + jnp.where(lane == ei, jnp.sum(jnp.where(match, trow, 0.0), axis=1, keepdims=True), 0.0)
        tok = tokm.T[:ne].astype(jnp.int32)
        row_ref[bi] = tok * SUBLANES
        pidx_ref[bi] = tok + t * lax.broadcasted_iota(jnp.int32, (ne, 1), 0)


def _route(logits_t, cap, *, nb=4):
    bsz, e, t = logits_t.shape
    return pl.pallas_call(
        functools.partial(_route_kernel, t=t, cap=cap),
        grid=(bsz // nb,),
        in_specs=[pl.BlockSpec((nb, e, t), lambda b: (b, 0, 0))],
        out_specs=[pl.BlockSpec((nb, e, cap), lambda b: (b, 0, 0)), pl.BlockSpec((nb, e, cap), lambda b: (b, 0, 0)),
                   pl.BlockSpec((nb, e, t), lambda b: (b, 0, 0))],
        out_shape=[jax.ShapeDtypeStruct((bsz, e, cap), jnp.int32), jax.ShapeDtypeStruct((bsz, e, cap), jnp.int32),
                   jax.ShapeDtypeStruct((bsz, e, t), F32)],
        scratch_shapes=[pltpu.VMEM((t, t), BF16)],
        compiler_params=_params("arbitrary"),
        name="route",
    )(logits_t)


def _gather_kernel(row_ref, hn_ref, x_ref, rows_scr, *, cap, ne):
    def per_expert(e, carry):
        for s in range(cap):
            start = pl.multiple_of(row_ref[0, 0, e * cap + s], SUBLANES)
            rows_scr[pl.ds(s * SUBLANES, SUBLANES), :] = hn_ref[0, pl.ds(start, SUBLANES), :]
        x_ref[e, 0] = jnp.concatenate(_from_token_major(rows_scr, cap), axis=-1).astype(BF16)
        return carry

    lax.fori_loop(0, ne, per_expert, 0)


def _gather(hn_tm, row, cap):
    bsz, rows, _ = hn_tm.shape
    e = row.shape[1]
    d = SUBLANES * LANES
    return pl.pallas_call(
        functools.partial(_gather_kernel, cap=cap, ne=e),
        grid=(bsz,),
        in_specs=[pl.BlockSpec((1, 1, e * cap), lambda b: (b, 0, 0), memory_space=pltpu.SMEM),
                  pl.BlockSpec((1, rows, LANES), lambda b: (b, 0, 0))],
        out_specs=pl.BlockSpec((e, 1, cap, d), lambda b: (0, b, 0, 0)),
        out_shape=jax.ShapeDtypeStruct((e, bsz, cap, d), BF16),
        scratch_shapes=[pltpu.VMEM((cap * SUBLANES, LANES), F32)],
        compiler_params=_params("parallel"),
        name="gather",
    )(row.reshape(bsz, 1, e * cap), hn_tm)


def _expert_kernel(x_ref, wg_ref, wu_ref, wd_ref, y_ref, wgb, wub, wdb):
    @pl.when(pl.program_id(1) == 0)
    def _():
        wgb[...] = wg_ref[0, 0].astype(BF16)
        wub[...] = wu_ref[0, 0].astype(BF16)
        wdb[...] = wd_ref[0, 0].astype(BF16)

    _, nb, cap, d = x_ref.shape
    xin = x_ref[0].reshape(nb * cap, d)
    hid = (_silu(_dot(xin, wgb[...])) * _dot(xin, wub[...])).astype(BF16)
    y = _dot(hid, wdb[...])
    half = cap // 2
    for bi in range(nb):
        hi = pltpu.bitcast(y[bi * cap:bi * cap + half].astype(BF16).astype(F32), jnp.int32)
        lo = pltpu.bitcast(y[bi * cap + half:(bi + 1) * cap].astype(BF16).astype(F32), jnp.int32)
        _to_token_major(y_ref.at[0, bi], hi | lax.shift_right_logical(lo, 16))


def _experts(xin, w_gate, w_up, w_down, layer, *, nb=2):
    e, bsz, cap, d = xin.shape
    ff = w_gate.shape[-1]
    return pl.pallas_call(
        _expert_kernel,
        grid=(e, bsz // nb),
        in_specs=[pl.BlockSpec((1, nb, cap, d), lambda ei, b: (ei, b, 0, 0)),
                  pl.BlockSpec((1, 1, d, ff), lambda ei, b: (layer, ei, 0, 0)),
                  pl.BlockSpec((1, 1, d, ff), lambda ei, b: (layer, ei, 0, 0)),
                  pl.BlockSpec((1, 1, ff, d), lambda ei, b: (layer, ei, 0, 0))],
        out_specs=pl.BlockSpec((1, nb, cap // 2 * SUBLANES, LANES), lambda ei, b: (ei, b, 0, 0)),
        out_shape=jax.ShapeDtypeStruct((e, bsz, cap // 2 * SUBLANES, LANES), jnp.int32),
        scratch_shapes=[pltpu.VMEM((d, ff), BF16), pltpu.VMEM((d, ff), BF16), pltpu.VMEM((ff, d), BF16)],
        compiler_params=_params("arbitrary", "arbitrary"),
        name="experts",
    )(xin, w_gate, w_up, w_down)


SCATTER_UNROLL = 8


def _scatter_kernel(row_ref, pidx_ref, p_ref, x_ref, y_ref, fw_ref, o_ref, acc_scr, *, cap, t, eg, final):
    g = pl.program_id(1)

    @pl.when(g == 0)
    def _():
        _to_token_major(acc_scr, x_ref[0])

    def per_expert(el, carry):
        e = g * eg + el
        half = cap // 2
        for s0 in range(0, half, SCATTER_UNROLL // 2):
            rows = []
            for s in range(s0, s0 + SCATTER_UNROLL // 2):
                word = y_ref[el, 0, pl.ds(s * SUBLANES, SUBLANES), :]
                pair = (pltpu.bitcast(word & -65536, F32), pltpu.bitcast(word << 16, F32))
                for slot, yrow in zip((s, s + half), pair):
                    aff = p_ref[0, 0, pidx_ref[0, 0, e * cap + slot]]
                    start = pl.multiple_of(row_ref[0, 0, e * cap + slot], SUBLANES)
                    rows.append((start, acc_scr[pl.ds(start, SUBLANES), :] + aff * yrow))
            for start, val in rows:
                acc_scr[pl.ds(start, SUBLANES), :] = val
        return carry

    lax.fori_loop(0, eg, per_expert, 0)

    @pl.when(g == pl.num_programs(1) - 1)
    def _():
        out = jnp.concatenate(_from_token_major(acc_scr, t), axis=-1)
        o_ref[0] = _rms(out, fw_ref[...]) if final else out


def _scatter(x, row, pidx, p, y, final_w, cap, *, final, eg=2):
    bsz, t, d = x.shape
    e = y.shape[0]
    smem = lambda n: pl.BlockSpec((1, 1, n), lambda b, g: (b, 0, 0), memory_space=pltpu.SMEM)
    return pl.pallas_call(
        functools.partial(_scatter_kernel, cap=cap, t=t, eg=eg, final=final),
        grid=(bsz, e // eg),
        in_specs=[smem(e * cap), smem(e * cap), smem(e * t),
                  pl.BlockSpec((1, t, d), lambda b, g: (b, 0, 0)),
                  pl.BlockSpec((eg, 1, cap // 2 * SUBLANES, LANES), lambda b, g: (g, b, 0, 0)),
                  pl.BlockSpec((1, d), lambda b, g: (0, 0))],
        out_specs=pl.BlockSpec((1, t, d), lambda b, g: (b, 0, 0)),
        out_shape=jax.ShapeDtypeStruct((bsz, t, d), F32),
        scratch_shapes=[pltpu.VMEM((t * SUBLANES, LANES), F32)],
        compiler_params=_params("parallel", "arbitrary"),
        name="scatter",
    )(row.reshape(bsz, 1, e * cap), pidx.reshape(bsz, 1, e * cap), p.reshape(bsz, 1, e * t), x, y, final_w[None])


def _moe(x1, hn, logits_t, w_gate, w_up, w_down, layer, final_w, *, final):
    t = x1.shape[1]
    cap = CAPACITY_FACTOR * t // N_EXPERTS
    row, pidx, p = _route(logits_t, cap)
    y = _experts(_gather(hn, row, cap), w_gate, w_up, w_down, layer)
    return _scatter(x1, row, pidx, p, y, final_w, cap, final=final)


def kernel(x, norm_mix, norm_ffn, dn_w_in, dn_conv_w, dn_a_log, dn_dt_bias, dn_norm_w, dn_w_out, cf_w_pw1, cf_b_pw1, cf_dw_w, cf_dw_b, cf_ln_g, cf_ln_b, cf_w_pw2, cf_b_pw2, moe_w_router, moe_w_gate, moe_w_up, moe_w_down, final_norm):
    q, k, v, z, g, grow = _gdn_pre(x, norm_mix[0], dn_w_in[0], dn_conv_w[0], dn_a_log[0], dn_dt_bias[0])
    og = _gdn_core(q, k, v, z, g, grow, dn_norm_w[0])
    x1, hn, lgt = _outproj_router(og, x, dn_w_out[0], norm_ffn[0], moe_w_router[0])
    x2 = _moe(x1, hn, lgt, moe_w_gate, moe_w_up, moe_w_down, 0, final_norm, final=False)
    x3, hn, lgt = _conformer_router(x2, norm_mix[1], cf_w_pw1[0], cf_b_pw1[0], cf_dw_w[0], cf_dw_b[0],
                                    cf_ln_g[0], cf_ln_b[0], cf_w_pw2[0], cf_b_pw2[0], norm_ffn[1],
                                    moe_w_router[1])
    return _moe(x3, hn, lgt, moe_w_gate, moe_w_up, moe_w_down, 1, final_norm, final=True)
```

```python
import functools

import jax
import jax.numpy as jnp
from jax import lax
from jax.experimental import pallas as pl
from jax.experimental.pallas import tpu as pltpu

F32 = jnp.float32
BF16 = jnp.bfloat16
EPS = 1e-6

LANES = 128
SUBLANES = 8
HEADS = 8
HEAD_DIM = 128
DN_CONV = 5
CF_CONV = 31
N_EXPERTS = 16
CAPACITY_FACTOR = 2
GDN_CHUNK = 128
GDN_HEADS_PER_STEP = 2
CONV_BLOCK = 16
VMEM_LIMIT = 56 * 1024 * 1024


def _params(*sem):
    return pltpu.CompilerParams(dimension_semantics=sem, vmem_limit_bytes=VMEM_LIMIT)


def _sigmoid(x):
    return 1.0 / (1.0 + jnp.exp(-x))


def _silu(x):
    return x * _sigmoid(x)


def _softplus(x):
    return jnp.maximum(x, 0.0) + jnp.log1p(jnp.exp(-jnp.abs(x)))


def _rms(x, w):
    return x * lax.rsqrt(jnp.mean(x * x, axis=-1, keepdims=True) + EPS) * w


def _dot(a, b):
    return jnp.dot(a, b, preferred_element_type=F32)


def _bdot(spec, a, b):
    return jnp.einsum(spec, a, b, preferred_element_type=F32)


def _seg_cumsum(x, seg, reverse):
    n = x.shape[0]
    r = lax.broadcasted_iota(jnp.int32, (n, 1), 0) % seg
    s = 1
    while s < seg:
        if reverse:
            x = x + jnp.where(r < seg - s, pltpu.roll(x, n - s, axis=0), 0.0)
        else:
            x = x + jnp.where(r >= s, pltpu.roll(x, s, axis=0), 0.0)
        s *= 2
    return x


def _halo_specs(tm, halo, nt_total, d):
    per = tm // halo
    prev = pl.BlockSpec((1, halo, d), lambda b, t: (b, jnp.maximum(t * per - 1, 0), 0))
    cur = pl.BlockSpec((1, tm, d), lambda b, t: (b, t, 0))
    nxt = pl.BlockSpec((1, halo, d), lambda b, t: (b, jnp.minimum((t + 1) * per, nt_total - 1), 0))
    return prev, cur, nxt


def _ext_valid(tm, halo, nt):
    t = pl.program_id(1)
    rows = lax.broadcasted_iota(jnp.int32, (tm + 2 * halo, 1), 0)
    lo = jnp.where(t == 0, halo, 0)
    hi = jnp.where(t == nt - 1, tm + halo, tm + 2 * halo)
    return (rows >= lo) & (rows < hi)


def _to_token_major(dst_ref, x, row0=0):
    rows = x.shape[0]
    for j in range(SUBLANES):
        dst_ref[pl.ds(row0 * SUBLANES + j, rows, stride=SUBLANES), :] = x[:, j * LANES:(j + 1) * LANES]


def _from_token_major(src_ref, rows):
    return [src_ref[pl.ds(j, rows, stride=SUBLANES), :] for j in range(SUBLANES)]


def _depthwise_conv_blocks(src_ref, dst_ref, tap, n_taps, first_row, tm, post):
    rows = CONV_BLOCK * SUBLANES
    for i in range(tm // CONV_BLOCK):
        acc = jnp.zeros((CONV_BLOCK, SUBLANES, LANES), F32)
        for j in range(n_taps):
            start = (i * CONV_BLOCK + first_row + j) * SUBLANES
            seg = src_ref[pl.ds(start, rows), :].reshape(CONV_BLOCK, SUBLANES, LANES)
            acc = acc + seg * tap(j)[None]
        dst_ref[pl.ds(i * rows, rows), :] = post(acc.reshape(rows, LANES))


def _gdn_pre_kernel(xp_ref, xc_ref, xn_ref, nw_ref, wqkvz_ref, wg_ref, cw_ref, alog_ref, dtb_ref,
                    q_ref, k_ref, v_ref, z_ref, g_ref, grow_ref, ext_ref, act_ref,
                    *, tm, nt, halo, chunk):
    width = HEADS * HEAD_DIM
    x_ext = jnp.concatenate([xp_ref[0], xc_ref[0], xn_ref[0]], axis=0)
    hn = _rms(x_ext, nw_ref[...]).astype(BF16)
    proj = _dot(hn, wqkvz_ref[...])
    valid = _ext_valid(tm, halo, nt)
    for h in range(HEADS):
        z_ref[0, h] = proj[halo:halo + tm, 3 * width + h * HEAD_DIM: 3 * width + (h + 1) * HEAD_DIM]

    l2 = lambda a: a * lax.rsqrt(jnp.sum(a * a, axis=-1, keepdims=True) + EPS)
    posts = (lambda a: l2(a) * (HEAD_DIM ** -0.5), l2, lambda a: a)
    for grp, out_ref in enumerate((q_ref, k_ref, v_ref)):
        _to_token_major(ext_ref, jnp.where(valid, proj[:, grp * width:(grp + 1) * width], 0.0))
        _depthwise_conv_blocks(ext_ref, act_ref, lambda j: cw_ref[j, grp], DN_CONV,
                               halo - DN_CONV // 2, tm, _silu)
        for h, tile in enumerate(_from_token_major(act_ref, tm)):
            out_ref[0, h] = posts[grp](tile).astype(BF16)

    graw = _dot(hn[halo:halo + tm], wg_ref[...])
    lane = lax.broadcasted_iota(jnp.int32, (1, LANES), 1)
    beta = _sigmoid(graw)
    logg = -jnp.exp(alog_ref[...]) * _softplus(graw + dtb_ref[...])
    cum_f = _seg_cumsum(logg, chunk, reverse=False)
    cum_b = _seg_cumsum(logg, chunk, reverse=True)
    is_beta = ((lane % (2 * HEADS)) < HEADS) & (lane < 4 * HEADS)
    is_f = (lane >= HEADS) & (lane < 2 * HEADS)
    is_b = (lane >= 3 * HEADS) & (lane < 4 * HEADS)
    g = jnp.where(is_beta, beta, jnp.where(is_f, cum_f, jnp.where(is_b, cum_b, 0.0)))
    g_ref[0] = g
    grow_ref[0] = g.T[:4 * HEADS]


def _gdn_pre(x, norm_w, w_in, conv_w, a_log, dt_bias, *, tm=512, halo=8):
    bsz, t, d = x.shape
    width = HEADS * HEAD_DIM
    nt = t // tm
    wqkvz = w_in[:, :4 * width].astype(BF16)
    wg = jnp.pad(w_in[:, 4 * width:], ((0, 0), (0, LANES - 4 * HEADS))).astype(BF16)
    cw = jnp.pad(conv_w, ((0, 8 - DN_CONV), (0, 0))).reshape(8, 3, HEADS, HEAD_DIM)
    zeros = jnp.zeros((HEADS,), F32)
    alog = jnp.pad(jnp.concatenate([zeros, a_log[0], zeros, a_log[1]]), (0, LANES - 4 * HEADS))[None]
    dtb = jnp.pad(jnp.concatenate([zeros, dt_bias[0], zeros, dt_bias[1]]), (0, LANES - 4 * HEADS))[None]
    prev, cur, nxt = _halo_specs(tm, halo, t // halo, d)
    full = lambda shape: pl.BlockSpec(shape, lambda b, tt: (0,) * len(shape))
    heads = pl.BlockSpec((1, HEADS, tm, HEAD_DIM), lambda b, tt: (b, 0, tt, 0))
    hshape = (bsz, HEADS, t, HEAD_DIM)
    return pl.pallas_call(
        functools.partial(_gdn_pre_kernel, tm=tm, nt=nt, halo=halo, chunk=GDN_CHUNK),
        grid=(bsz, nt),
        in_specs=[prev, cur, nxt, full((1, d)), full((d, 4 * width)), full((d, LANES)),
                  full((8, 3, HEADS, HEAD_DIM)), full((1, LANES)), full((1, LANES))],
        out_specs=[heads, heads, heads, heads,
                   pl.BlockSpec((1, tm, LANES), lambda b, tt: (b, tt, 0)),
                   pl.BlockSpec((1, 4 * HEADS, tm), lambda b, tt: (b, 0, tt))],
        out_shape=[jax.ShapeDtypeStruct(hshape, BF16)] * 3
        + [jax.ShapeDtypeStruct(hshape, F32), jax.ShapeDtypeStruct((bsz, t, LANES), F32),
           jax.ShapeDtypeStruct((bsz, 4 * HEADS, t), F32)],
        scratch_shapes=[pltpu.VMEM(((tm + 2 * halo) * SUBLANES, LANES), F32),
                        pltpu.VMEM((tm * SUBLANES, LANES), F32)],
        compiler_params=_params("parallel", "parallel"),
        name="gdn_pre",
    )(x, x, x, norm_w[None], wqkvz, wg, cw, alog, dtb)


def _unit_tri_inverse(a, c):
    ii = lax.broadcasted_iota(jnp.int32, (c, c), 0)
    jj = lax.broadcasted_iota(jnp.int32, (c, c), 1)
    eye = (ii == jj).astype(F32)
    x = eye - jnp.where((ii // 2) == (jj // 2), a, 0.0)
    s = 2
    while s < c:
        off = ((ii // (2 * s)) == (jj // (2 * s))) & ((ii // s) != (jj // s))
        a_l = jnp.where(off, a, 0.0).astype(BF16)
        y = _bdot('nij,njk->nik', x.astype(BF16), a_l)
        x = x - _bdot('nij,njk->nik', y.astype(BF16), x.astype(BF16))
        s *= 2
    return x


def _gdn_core_kernel(q_ref, k_ref, v_ref, z_ref, g_ref, grow_ref, nw_ref, o_ref,
                     mf_scr, mb_scr, nf_scr, nb_scr, qpf_scr, qpb_scr, sf_scr, sb_scr, oacc_scr,
                     *, t, c, hb):
    n = t // c
    d = HEAD_DIM
    lane = lax.broadcasted_iota(jnp.int32, (1, LANES), 1)
    ii = lax.broadcasted_iota(jnp.int32, (c, c), 0)
    jj = lax.broadcasted_iota(jnp.int32, (c, c), 1)
    eye_d = (lax.broadcasted_iota(jnp.int32, (d, d), 0) == lax.broadcasted_iota(jnp.int32, (d, d), 1)).astype(F32)
    m_scr, n_scr, qp_scr = (mf_scr, mb_scr), (nf_scr, nb_scr), (qpf_scr, qpb_scr)

    def precompute(hh, carry):
        h = pl.program_id(1) * hb + hh
        gt = g_ref[0]
        kb = k_ref[0, hh].reshape(n, c, d)
        qb = q_ref[0, hh].reshape(n, c, d)
        kf = kb.astype(F32)
        qf = qb.astype(F32)
        vf = v_ref[0, hh].astype(F32).reshape(n, c, d)
        kk = _bdot('ncd,nsd->ncs', kb, kb)
        qk = _bdot('ncd,nsd->ncs', qb, kb)
        for di in range(2):
            col = lambda idx: jnp.sum(jnp.where(lane == idx * HEADS + h, gt, 0.0), axis=-1,
                                      keepdims=True).reshape(n, c, 1)
            beta = col(2 * di)
            gcol = col(2 * di + 1)
            grow = grow_ref[0, 2 * di + 1, hh]
            incl = (ii >= jj) if di == 0 else (ii <= jj)
            strict = (ii > jj) if di == 0 else (ii < jj)
            decay = jnp.where(incl, jnp.exp(jnp.where(incl, gcol - grow, 0.0)), 0.0)
            a = jnp.where(strict, beta * kk * decay, 0.0)
            tinv = _unit_tri_inverse(a, c)
            egc = jnp.exp(gcol)
            rhs = jnp.concatenate([vf * beta, kf * (beta * egc)], axis=-1).astype(BF16)
            uw = _bdot('ncs,nsd->ncd', tinv.astype(BF16), rhs).astype(BF16)
            iw = _bdot('ncs,nsd->ncd', (qk * decay).astype(BF16), uw)
            qp = qf * egc - iw[..., d:]
            glast = gcol[:, c - 1:c, :] if di == 0 else gcol[:, 0:1, :]
            kt = (kf * jnp.exp(glast - gcol)).astype(BF16)
            ktuw = _bdot('ncd,nce->nde', kt, uw)
            m_scr[di][hh] = (jnp.exp(glast) * eye_d - ktuw[..., d:]).astype(BF16)
            n_scr[di][hh] = ktuw[..., :d]
            qp_scr[di][hh] = qp.reshape(t, d).astype(BF16)
            if di == 0:
                oacc_scr[hh] = iw[..., :d].reshape(t, d)
            else:
                oacc_scr[hh] += iw[..., :d].reshape(t, d)
        return carry

    lax.fori_loop(0, hb, precompute, 0)

    for hh in range(hb):
        sf_scr[hh, 0] = jnp.zeros((d, d), BF16)
        sb_scr[hh, n - 1] = jnp.zeros((d, d), BF16)

    def scan(i, carry):
        nb = n - 1 - i
        ins = [(mf_scr[hh, i], sf_scr[hh, i], nf_scr[hh, i], mb_scr[hh, nb], sb_scr[hh, nb], nb_scr[hh, nb])
               for hh in range(hb)]
        outs = [((_dot(mf, sf) + nf).astype(BF16), (_dot(mb, sb) + nbv).astype(BF16))
                for mf, sf, nf, mb, sb, nbv in ins]
        for hh, (of, ob) in enumerate(outs):
            sf_scr[hh, i + 1] = of
            sb_scr[hh, nb - 1] = ob
        return carry

    lax.fori_loop(0, n - 1, scan, 0)

    def finish(hh, carry):
        qp = jnp.concatenate([qpf_scr[hh], qpb_scr[hh]], axis=-1).reshape(n, c, 2 * d)
        st = jnp.concatenate([sf_scr[hh], sb_scr[hh]], axis=-2)
        o = oacc_scr[hh] + _bdot('nck,nkd->ncd', qp, st).reshape(t, d)
        o_ref[0, hh] = (_rms(o, nw_ref[...]) * _silu(z_ref[0, hh])).astype(BF16)
        return carry

    lax.fori_loop(0, hb, finish, 0)


def _gdn_core(q, k, v, z, g, grow, norm_w):
    bsz, _, t, d = q.shape
    c, hb = GDN_CHUNK, GDN_HEADS_PER_STEP
    n = t // c
    grow = grow.reshape(bsz, 4, HEADS, n, 1, c)
    head = pl.BlockSpec((1, hb, t, d), lambda b, h: (b, h, 0, 0))
    mats = lambda dt: pltpu.VMEM((hb, n, d, d), dt)
    return pl.pallas_call(
        functools.partial(_gdn_core_kernel, t=t, c=c, hb=hb),
        grid=(bsz, HEADS // hb),
        in_specs=[head, head, head, head,
                  pl.BlockSpec((1, t, LANES), lambda b, h: (b, 0, 0)),
                  pl.BlockSpec((1, 4, hb, n, 1, c), lambda b, h: (b, 0, h, 0, 0, 0)),
                  pl.BlockSpec((1, d), lambda b, h: (0, 0))],
        out_specs=head,
        out_shape=jax.ShapeDtypeStruct((bsz, HEADS, t, d), BF16),
        scratch_shapes=[mats(BF16), mats(BF16), mats(F32), mats(F32),
                        pltpu.VMEM((hb, t, d), BF16), pltpu.VMEM((hb, t, d), BF16),
                        mats(BF16), mats(BF16), pltpu.VMEM((hb, t, d), F32)],
        compiler_params=_params("parallel", "parallel"),
        name="gdn_core",
    )(q, k, v, z, g, grow, norm_w[None])


def _router_epilogue(x1, r0, nffn_ref, wr_ref, hn_ref, lgt_ref):
    hn = _rms(x1, nffn_ref[...])
    hi = hn.astype(BF16)
    lo = (hn - hi.astype(F32)).astype(BF16)
    _to_token_major(hn_ref.at[0], hn, r0)
    lg = _dot(hi, wr_ref[0]) + _dot(hi, wr_ref[1]) + _dot(lo, wr_ref[0])
    lgt_ref[0, :, pl.ds(r0, x1.shape[0])] = lg.T[:N_EXPERTS]


def _router_weights(w_router):
    wp = jnp.pad(w_router, ((0, 0), (0, LANES - N_EXPERTS)))
    hi = wp.astype(BF16)
    lo = (wp - hi.astype(F32)).astype(BF16)
    return jnp.stack([hi, lo])


OUTPROJ_ROW_BLOCK = 256


def _outproj_kernel(og_ref, x_ref, wout_ref, nffn_ref, wr_ref, x1_ref, hn_ref, lgt_ref):
    rb = OUTPROJ_ROW_BLOCK
    for r0 in range(0, x_ref.shape[1], rb):
        og = jnp.concatenate([og_ref[0, h, pl.ds(r0, rb), :] for h in range(HEADS)], axis=-1)
        x1 = x_ref[0, pl.ds(r0, rb), :] + _dot(og, wout_ref[...])
        x1_ref[0, pl.ds(r0, rb), :] = x1
        _router_epilogue(x1, r0, nffn_ref, wr_ref, hn_ref, lgt_ref)


def _router_out(bsz, t, d, tm):
    rows = d // LANES
    specs = [pl.BlockSpec((1, tm, d), lambda b, tt: (b, tt, 0)),
             pl.BlockSpec((1, tm * rows, LANES), lambda b, tt: (b, tt, 0)),
             pl.BlockSpec((1, N_EXPERTS, tm), lambda b, tt: (b, 0, tt))]
    shapes = [jax.ShapeDtypeStruct((bsz, t, d), F32), jax.ShapeDtypeStruct((bsz, t * rows, LANES), F32),
              jax.ShapeDtypeStruct((bsz, N_EXPERTS, t), F32)]
    return specs, shapes


def _outproj_router(og, x, w_out, norm_ffn, w_router, *, tm=512):
    bsz, t, d = x.shape
    full = lambda shape: pl.BlockSpec(shape, lambda b, tt: (0,) * len(shape))
    out_specs, out_shape = _router_out(bsz, t, d, tm)
    return pl.pallas_call(
        _outproj_kernel,
        grid=(bsz, t // tm),
        in_specs=[pl.BlockSpec((1, HEADS, tm, HEAD_DIM), lambda b, tt: (b, 0, tt, 0)),
                  pl.BlockSpec((1, tm, d), lambda b, tt: (b, tt, 0)),
                  full(w_out.shape), full((1, d)), full((2, d, LANES))],
        out_specs=out_specs,
        out_shape=out_shape,
        compiler_params=_params("parallel", "parallel"),
        name="outproj_router",
    )(og, x, w_out.astype(BF16), norm_ffn[None], _router_weights(w_router))


def _conformer_kernel(xp_ref, xc_ref, xn_ref, nmix_ref, w1_ref, b1_ref, dww_ref, dwb_ref, lng_ref, lnb_ref,
                      w2_ref, b2_ref, nffn_ref, wr_ref, x1_ref, hn_ref, lgt_ref, u_scr, c_scr,
                      *, tm, nt, halo):
    d = xc_ref.shape[-1]
    x_ext = jnp.concatenate([xp_ref[0], xc_ref[0], xn_ref[0]], axis=0)
    hn = _rms(x_ext, nmix_ref[...]).astype(BF16)
    pg = _dot(hn, w1_ref[...]) + b1_ref[...]
    u = pg[:, :d] * _sigmoid(pg[:, d:])
    _to_token_major(u_scr, jnp.where(_ext_valid(tm, halo, nt), u, 0.0))
    _depthwise_conv_blocks(u_scr, c_scr, lambda j: dww_ref[j], CF_CONV, halo - CF_CONV // 2, tm, lambda a: a)
    conv = jnp.concatenate(_from_token_major(c_scr, tm), axis=-1) + dwb_ref[...]
    xc = conv - jnp.mean(conv, axis=-1, keepdims=True)
    y = xc * lax.rsqrt(jnp.mean(xc * xc, axis=-1, keepdims=True) + EPS) * lng_ref[...] + lnb_ref[...]
    mix = _dot(_silu(y).astype(BF16), w2_ref[...]) + b2_ref[...]
    x1 = xc_ref[0] + mix
    x1_ref[0] = x1
    _router_epilogue(x1, 0, nffn_ref, wr_ref, hn_ref, lgt_ref)


def _conformer_router(x, norm_mix, w_pw1, b_pw1, dw_w, dw_b, ln_g, ln_b, w_pw2, b_pw2, norm_ffn, w_router,
                      *, tm=256, halo=16):
    bsz, t, d = x.shape
    nt = t // tm
    prev, cur, nxt = _halo_specs(tm, halo, t // halo, d)
    full = lambda shape: pl.BlockSpec(shape, lambda b, tt: (0,) * len(shape))
    dww = jnp.pad(dw_w, ((0, 32 - CF_CONV), (0, 0))).reshape(32, SUBLANES, LANES)
    out_specs, out_shape = _router_out(bsz, t, d, tm)
    return pl.pallas_call(
        functools.partial(_conformer_kernel, tm=tm, nt=nt, halo=halo),
        grid=(bsz, nt),
        in_specs=[prev, cur, nxt, full((1, d)), full((d, 2 * d)), full((1, 2 * d)),
                  full((32, SUBLANES, LANES)), full((1, d)), full((1, d)), full((1, d)), full((d, d)),
                  full((1, d)), full((1, d)), full((2, d, LANES))],
        out_specs=out_specs,
        out_shape=out_shape,
        scratch_shapes=[pltpu.VMEM(((tm + 2 * halo) * SUBLANES, LANES), F32),
                        pltpu.VMEM((tm * SUBLANES, LANES), F32)],
        compiler_params=_params("parallel", "parallel"),
        name="conformer_router",
    )(x, x, x, norm_mix[None], w_pw1.astype(BF16), b_pw1[None], dww, dw_b[None], ln_g[None], ln_b[None],
      w_pw2.astype(BF16), b_pw2[None], norm_ffn[None], _router_weights(w_router))


def _route_kernel(lg_ref, row_ref, pidx_ref, p_ref, tri_scr, *, t, cap):
    @pl.when(pl.program_id(0) == 0)
    def _():
        r = lax.broadcasted_iota(jnp.int32, (t, t), 0)
        cidx = lax.broadcasted_iota(jnp.int32, (t, t), 1)
        tri_scr[...] = jnp.where(r < cidx, 1.0, 0.0).astype(BF16)

    count = lambda m: jnp.sum(jnp.where(m, 1.0, 0.0), axis=1, keepdims=True)
    trow = lax.broadcasted_iota(jnp.int32, (1, t), 1).astype(F32)
    slot_sub = lax.broadcasted_iota(jnp.int32, (cap, 1), 0)
    lane = lax.broadcasted_iota(jnp.int32, (1, LANES), 1)
    for bi in range(lg_ref.shape[0]):
        lt = lg_ref[bi]
        ne = lt.shape[0]
        e = jnp.exp(lt - jnp.max(lt, axis=0, keepdims=True))
        p = e / jnp.sum(e, axis=0, keepdims=True)
        p_ref[bi] = p
        bits = pltpu.bitcast(p, jnp.int32)
        thr = jnp.zeros((ne, 1), jnp.int32)
        for bit in range(30, -1, -1):
            cand = thr | (1 << bit)
            thr = jnp.where(count(bits >= cand) >= cap, cand, thr)
        gt = bits > thr
        eq = bits == thr
        need = cap - count(gt)
        eq_before = _dot(jnp.where(eq, 1.0, 0.0).astype(BF16), tri_scr[...])
        sel = gt | (eq & (eq_before < need))
        slot = _dot(jnp.where(sel, 1.0, 0.0).astype(BF16), tri_scr[...])
        pos = jnp.where(sel, slot.astype(jnp.int32), -1)
        tokm = jnp.zeros((cap, LANES), F32)
        for ei in range(ne):
            match = slot_sub == pos[ei:ei + 1, :]
            tokm = tokm + jnp.where(lane == ei, jnp.sum(jnp.where(match, trow, 0.0), axis=1, keepdims=True), 0.0)
        tok = tokm.T[:ne].astype(jnp.int32)
        row_ref[bi] = tok * SUBLANES
        pidx_ref[bi] = tok + t * lax.broadcasted_iota(jnp.int32, (ne, 1), 0)


def _route(logits_t, cap, *, nb=4):
    bsz, e, t = logits_t.shape
    return pl.pallas_call(
        functools.partial(_route_kernel, t=t, cap=cap),
        grid=(bsz // nb,),
        in_specs=[pl.BlockSpec((nb, e, t), lambda b: (b, 0, 0))],
        out_specs=[pl.BlockSpec((nb, e, cap), lambda b: (b, 0, 0)), pl.BlockSpec((nb, e, cap), lambda b: (b, 0, 0)),
                   pl.BlockSpec((nb, e, t), lambda b: (b, 0, 0))],
        out_shape=[jax.ShapeDtypeStruct((bsz, e, cap), jnp.int32), jax.ShapeDtypeStruct((bsz, e, cap), jnp.int32),
                   jax.ShapeDtypeStruct((bsz, e, t), F32)],
        scratch_shapes=[pltpu.VMEM((t, t), BF16)],
        compiler_params=_params("arbitrary"),
        name="route",
    )(logits_t)


def _gather_kernel(row_ref, hn_ref, x_ref, rows_scr, *, cap, ne):
    def per_expert(e, carry):
        for s in range(cap):
            start = pl.multiple_of(row_ref[0, 0, e * cap + s], SUBLANES)
            rows_scr[pl.ds(s * SUBLANES, SUBLANES), :] = hn_ref[0, pl.ds(start, SUBLANES), :]
        x_ref[e, 0] = jnp.concatenate(_from_token_major(rows_scr, cap), axis=-1).astype(BF16)
        return carry

    lax.fori_loop(0, ne, per_expert, 0)


def _gather(hn_tm, row, cap):
    bsz, rows, _ = hn_tm.shape
    e = row.shape[1]
    d = SUBLANES * LANES
    return pl.pallas_call(
        functools.partial(_gather_kernel, cap=cap, ne=e),
        grid=(bsz,),
        in_specs=[pl.BlockSpec((1, 1, e * cap), lambda b: (b, 0, 0), memory_space=pltpu.SMEM),
                  pl.BlockSpec((1, rows, LANES), lambda b: (b, 0, 0))],
        out_specs=pl.BlockSpec((e, 1, cap, d), lambda b: (0, b, 0, 0)),
        out_shape=jax.ShapeDtypeStruct((e, bsz, cap, d), BF16),
        scratch_shapes=[pltpu.VMEM((cap * SUBLANES, LANES), F32)],
        compiler_params=_params("parallel"),
        name="gather",
    )(row.reshape(bsz, 1, e * cap), hn_tm)


def _expert_kernel(x_ref, wg_ref, wu_ref, wd_ref, y_ref, wgb, wub, wdb):
    @pl.when(pl.program_id(1) == 0)
    def _():
        wgb[...] = wg_ref[0, 0].astype(BF16)
        wub[...] = wu_ref[0, 0].astype(BF16)
        wdb[...] = wd_ref[0, 0].astype(BF16)

    _, nb, cap, d = x_ref.shape
    xin = x_ref[0].reshape(nb * cap, d)
    hid = (_silu(_dot(xin, wgb[...])) * _dot(xin, wub[...])).astype(BF16)
    y = _dot(hid, wdb[...])
    half = cap // 2
    for bi in range(nb):
        hi = pltpu.bitcast(y[bi * cap:bi * cap + half].astype(BF16).astype(F32), jnp.int32)
        lo = pltpu.bitcast(y[bi * cap + half:(bi + 1) * cap].astype(BF16).astype(F32), jnp.int32)
        _to_token_major(y_ref.at[0, bi], hi | lax.shift_right_logical(lo, 16))


def _experts(xin, w_gate, w_up, w_down, layer, *, nb=2):
    e, bsz, cap, d = xin.shape
    ff = w_gate.shape[-1]
    return pl.pallas_call(
        _expert_kernel,
        grid=(e, bsz // nb),
        in_specs=[pl.BlockSpec((1, nb, cap, d), lambda ei, b: (ei, b, 0, 0)),
                  pl.BlockSpec((1, 1, d, ff), lambda ei, b: (layer, ei, 0, 0)),
                  pl.BlockSpec((1, 1, d, ff), lambda ei, b: (layer, ei, 0, 0)),
                  pl.BlockSpec((1, 1, ff, d), lambda ei, b: (layer, ei, 0, 0))],
        out_specs=pl.BlockSpec((1, nb, cap // 2 * SUBLANES, LANES), lambda ei, b: (ei, b, 0, 0)),
        out_shape=jax.ShapeDtypeStruct((e, bsz, cap // 2 * SUBLANES, LANES), jnp.int32),
        scratch_shapes=[pltpu.VMEM((d, ff), BF16), pltpu.VMEM((d, ff), BF16), pltpu.VMEM((ff, d), BF16)],
        compiler_params=_params("arbitrary", "arbitrary"),
        name="experts",
    )(xin, w_gate, w_up, w_down)


SCATTER_UNROLL = 8


def _scatter_kernel(row_ref, pidx_ref, p_ref, x_ref, y_ref, fw_ref, o_ref, acc_scr, *, cap, t, eg, final):
    g = pl.program_id(1)

    @pl.when(g == 0)
    def _():
        _to_token_major(acc_scr, x_ref[0])

    def per_expert(el, carry):
        e = g * eg + el
        half = cap // 2
        for s0 in range(0, half, SCATTER_UNROLL // 2):
            rows = []
            for s in range(s0, s0 + SCATTER_UNROLL // 2):
                word = y_ref[el, 0, pl.ds(s * SUBLANES, SUBLANES), :]
                pair = (pltpu.bitcast(word & -65536, F32), pltpu.bitcast(word << 16, F32))
                for slot, yrow in zip((s, s + half), pair):
                    aff = p_ref[0, 0, pidx_ref[0, 0, e * cap + slot]]
                    start = pl.multiple_of(row_ref[0, 0, e * cap + slot], SUBLANES)
                    rows.append((start, acc_scr[pl.ds(start, SUBLANES), :] + aff * yrow))
            for start, val in rows:
                acc_scr[pl.ds(start, SUBLANES), :] = val
        return carry

    lax.fori_loop(0, eg, per_expert, 0)

    @pl.when(g == pl.num_programs(1) - 1)
    def _():
        out = jnp.concatenate(_from_token_major(acc_scr, t), axis=-1)
        o_ref[0] = _rms(out, fw_ref[...]) if final else out


def _scatter(x, row, pidx, p, y, final_w, cap, *, final, eg=2):
    bsz, t, d = x.shape
    e = y.shape[0]
    smem = lambda n: pl.BlockSpec((1, 1, n), lambda b, g: (b, 0, 0), memory_space=pltpu.SMEM)
    return pl.pallas_call(
        functools.partial(_scatter_kernel, cap=cap, t=t, eg=eg, final=final),
        grid=(bsz, e // eg),
        in_specs=[smem(e * cap), smem(e * cap), smem(e * t),
                  pl.BlockSpec((1, t, d), lambda b, g: (b, 0, 0)),
                  pl.BlockSpec((eg, 1, cap // 2 * SUBLANES, LANES), lambda b, g: (g, b, 0, 0)),
                  pl.BlockSpec((1, d), lambda b, g: (0, 0))],
        out_specs=pl.BlockSpec((1, t, d), lambda b, g: (b, 0, 0)),
        out_shape=jax.ShapeDtypeStruct((bsz, t, d), F32),
        scratch_shapes=[pltpu.VMEM((t * SUBLANES, LANES), F32)],
        compiler_params=_params("parallel", "arbitrary"),
        name="scatter",
    )(row.reshape(bsz, 1, e * cap), pidx.reshape(bsz, 1, e * cap), p.reshape(bsz, 1, e * t), x, y, final_w[None])


def _moe(x1, hn, logits_t, w_gate, w_up, w_down, layer, final_w, *, final):
    t = x1.shape[1]
    cap = CAPACITY_FACTOR * t // N_EXPERTS
    row, pidx, p = _route(logits_t, cap)
    y = _experts(_gather(hn, row, cap), w_gate, w_up, w_down, layer)
    return _scatter(x1, row, pidx, p, y, final_w, cap, final=final)


def kernel(x, norm_mix, norm_ffn, dn_w_in, dn_conv_w, dn_a_log, dn_dt_bias, dn_norm_w, dn_w_out, cf_w_pw1, cf_b_pw1, cf_dw_w, cf_dw_b, cf_ln_g, cf_ln_b, cf_w_pw2, cf_b_pw2, moe_w_router, moe_w_gate, moe_w_up, moe_w_down, final_norm):
    q, k, v, z, g, grow = _gdn_pre(x, norm_mix[0], dn_w_in[0], dn_conv_w[0], dn_a_log[0], dn_dt_bias[0])
    og = _gdn_core(q, k, v, z, g, grow, dn_norm_w[0])
    x1, hn, lgt = _outproj_router(og, x, dn_w_out[0], norm_ffn[0], moe_w_router[0])
    x2 = _moe(x1, hn, lgt, moe_w_gate, moe_w_up, moe_w_down, 0, final_norm, final=False)
    x3, hn, lgt = _conformer_router(x2, norm_mix[1], cf_w_pw1[0], cf_b_pw1[0], cf_dw_w[0], cf_dw_b[0],
                                    cf_ln_g[0], cf_ln_b[0], cf_w_pw2[0], cf_b_pw2[0], norm_ffn[1],
                                    moe_w_router[1])
    return _moe(x3, hn, lgt, moe_w_gate, moe_w_up, moe_w_down, 1, final_norm, final=True)
```
